```python
import math
import jax, jax.numpy as jnp
from jax import lax
import numpy as np

D_MODEL = 2048
BATCH = 2
SEQ = 4096
DEPTH = 2
DEC_BATCH = 128
DEC_SEQ = 4
PAST_LEN = 8192
PAGE_SIZE = 128

NSA_HEADS = 4
NSA_HD = 128
CMP_LEN = 32
CMP_STRIDE = 16
SEL_BLOCK = 64
N_SEL = 16
WINDOW = 512
MLA_HEADS = 8
MLA_NOPE = 128
MLA_ROPE = 64
MLA_VD = 128
Q_LORA = 512
KV_LORA = 256
MLA_ROW = KV_LORA + MLA_ROPE + MLA_HEADS
DSA_HEADS = 4
DSA_HD = 128
IDX_HEADS = 16
IDX_DIM = 64
IDX_TOPK = 256
MIX_WIDTH = NSA_HEADS * NSA_HD + MLA_HEADS * MLA_VD + DSA_HEADS * DSA_HD
D_FF = 4 * D_MODEL
NUM_BUCKETS = 32
MAX_DISTANCE = 128
ROPE_THETA = 10000.0
QBLK = 128
EPS = 1e-6
NEG = -1e30
FORCE_BONUS = 1e4

PROJ_SPLITS = (
    ('nsa_q', NSA_HEADS * NSA_HD),
    ('nsa_cmp_kv', 2 * NSA_HD),
    ('nsa_slc_kv', 2 * NSA_HD),
    ('nsa_win_kv', 2 * NSA_HD),
    ('nsa_gate', 3 * NSA_HEADS),
    ('mla_dq', Q_LORA),
    ('mla_dkv', KV_LORA),
    ('mla_kpe', MLA_ROPE),
    ('dsa_q', DSA_HEADS * DSA_HD),
    ('dsa_kv', 2 * DSA_HD),
    ('dsa_qidx', IDX_HEADS * IDX_DIM),
    ('dsa_kidx', IDX_DIM),
    ('dsa_w', IDX_HEADS),
)
PROJ_WIDTH = sum(w for _, w in PROJ_SPLITS)

kernel_name = 'hybrid_nsa_mla_dsa_decoder_step'


def rms_norm(x, g):
    xf = x.astype(jnp.float32)
    y = xf * lax.rsqrt(jnp.mean(xf * xf, axis=-1, keepdims=True) + EPS)
    return (y * g.astype(jnp.float32)).astype(x.dtype)


def norm_key(kv, g):
    return jnp.stack([rms_norm(kv[..., 0, :], g), kv[..., 1, :]], axis=-2)


def rope(x, pos):
    half = x.shape[-1] // 2
    freq = ROPE_THETA ** (-jnp.arange(half, dtype=jnp.float32) / half)
    ang = pos.astype(jnp.float32)[:, None] * freq[None]
    ang = ang.reshape((1, pos.shape[0]) + (1,) * (x.ndim - 3) + (half,))
    cos, sin = jnp.cos(ang), jnp.sin(ang)
    xf = x.astype(jnp.float32)
    x1, x2 = xf[..., :half], xf[..., half:]
    return jnp.concatenate([x1 * cos - x2 * sin, x1 * sin + x2 * cos], -1).astype(x.dtype)


def t5_bucket(dist):
    n = jnp.maximum(dist, 0)
    max_exact = NUM_BUCKETS // 2
    nf = jnp.maximum(n, 1).astype(jnp.float32)
    large = max_exact + (jnp.log(nf / max_exact) / math.log(MAX_DISTANCE / max_exact)
                         * (NUM_BUCKETS - max_exact)).astype(jnp.int32)
    return jnp.where(n < max_exact, n, jnp.minimum(large, NUM_BUCKETS - 1))


def masked_softmax(s, mask):
    s = jnp.where(mask, s.astype(jnp.float32), NEG)
    e = jnp.where(mask, jnp.exp(s - jnp.max(s, axis=-1, keepdims=True)), 0.0)
    return e / jnp.maximum(jnp.sum(e, axis=-1, keepdims=True), 1e-30)


def split_proj(p):
    out, off = {}, 0
    for name, w in PROJ_SPLITS:
        out[name] = p[..., off:off + w]
        off += w
    return out


def map_query_blocks(fn, qpos, *xs):
    T = qpos.shape[0]
    if T <= QBLK or T % QBLK:
        return fn(qpos, *xs)
    nb = T // QBLK
    blk = lambda a: jnp.moveaxis(a.reshape((a.shape[0], nb, QBLK) + a.shape[2:]), 1, 0)
    out = lax.map(lambda a: fn(*a), (qpos.reshape(nb, QBLK),) + tuple(blk(a) for a in xs))
    out = jnp.moveaxis(out, 0, 1)
    return out.reshape((out.shape[0], T) + out.shape[3:])


def gather_past(pool, page_table):
    g = pool[page_table]
    return g.reshape((page_table.shape[0], -1) + pool.shape[2:])


def gather_rows_paged(pool, page_table, new_rows, tok):
    past = page_table.shape[1] * PAGE_SIZE
    Bd = tok.shape[0]
    flat = tok.reshape(Bd, -1)
    tp = jnp.clip(flat, 0, past - 1)
    phys = jnp.take_along_axis(page_table, tp // PAGE_SIZE, axis=1)
    from_pool = pool[phys, tp % PAGE_SIZE]
    from_new = new_rows[jnp.arange(Bd)[:, None], jnp.clip(flat - past, 0, new_rows.shape[1] - 1)]
    in_past = (flat < past).reshape(flat.shape + (1,) * (pool.ndim - 2))
    return jnp.where(in_past, from_pool, from_new).reshape(tok.shape + pool.shape[2:])


def attend_shared(q, kv, dist, mask, bias_tab):
    s = jnp.einsum('bthd,bnd->bthn', q, kv[:, :, 0]).astype(jnp.float32) * q.shape[-1] ** -0.5
    s = s + jnp.swapaxes(bias_tab[t5_bucket(dist)], -1, -2).astype(jnp.float32)
    p = masked_softmax(s, mask[None, :, None, :])
    return jnp.einsum('bthn,bnd->bthd', p.astype(q.dtype), kv[:, :, 1]), p


def attend_gathered(q, rows, kpos, qpos, bias_tab):
    dist = qpos[None, :, None] - kpos
    s = jnp.einsum('bthd,btnd->bthn', q, rows[..., 0, :]).astype(jnp.float32) * q.shape[-1] ** -0.5
    s = s + jnp.swapaxes(bias_tab[t5_bucket(dist)], -1, -2).astype(jnp.float32)
    p = masked_softmax(s, (dist >= 0)[:, :, None, :])
    return jnp.einsum('bthn,btnd->bthd', p.astype(q.dtype), rows[..., 1, :])


def nsa_compress(kv, w_cmp):
    B, L = kv.shape[:2]
    Lp = -(-L // CMP_STRIDE) * CMP_STRIDE
    ch = jnp.pad(kv, ((0, 0), (0, Lp - L), (0, 0), (0, 0)))
    ch = ch.reshape(B, Lp // CMP_STRIDE, CMP_STRIDE, 2, kv.shape[-1])
    head = jnp.einsum('bcjkd,kj->bckd', ch, w_cmp[:, :CMP_STRIDE])
    tail = jnp.einsum('bcjkd,kj->bckd', ch, w_cmp[:, CMP_STRIDE:])
    return head[:, :-1] + tail[:, 1:]


def nsa_cmp_branch(q, qpos, comp, g_kc, bias_a, L):
    comp = norm_key(comp, g_kc)
    NC = comp.shape[1]
    cstart = jnp.arange(NC) * CMP_STRIDE
    dist = qpos[:, None] - (cstart + CMP_LEN - 1)[None]
    o, p = attend_shared(q, comp, dist, dist >= 0, bias_a)
    NS = -(-L // SEL_BLOCK)
    sstart = jnp.arange(NS) * SEL_BLOCK
    shared_tok = jnp.maximum(jnp.minimum(cstart[:, None] + CMP_LEN, sstart[None] + SEL_BLOCK)
                             - jnp.maximum(cstart[:, None], sstart[None]), 0)
    cover = shared_tok.astype(jnp.float32) / CMP_LEN
    imp = jnp.einsum('btn,ns->bts', jnp.sum(p, axis=2), cover)
    blk = jnp.arange(NS)[None]
    qb = (qpos // SEL_BLOCK)[:, None]
    forced = (blk == 0) | (blk == qb) | (blk == qb - 1)
    score = jnp.where(blk <= qb, imp + jnp.where(forced, FORCE_BONUS, 0.0), -jnp.inf)
    _, idx = lax.top_k(score, min(N_SEL, NS))
    return o, idx


def nsa_slc_branch(qpos, q, idx, get_rows, bias_a):
    tok = idx[..., None] * SEL_BLOCK + jnp.arange(SEL_BLOCK)
    tok = tok.reshape(idx.shape[0], idx.shape[1], -1)
    return attend_gathered(q, get_rows(tok), tok, qpos, bias_a)


def nsa_win_prompt(q, kv, bias_a):
    B, S, H, D = q.shape
    nb, back = S // QBLK, WINDOW // QBLK
    kvb = jnp.pad(kv.reshape(B, nb, QBLK, 2, D), ((0, 0), (back, 0), (0, 0), (0, 0), (0, 0)))
    band = jnp.concatenate([kvb[:, j:j + nb] for j in range(back + 1)], axis=2)
    qpos = jnp.arange(S).reshape(nb, QBLK)
    kpos = (jnp.arange(nb)[:, None] - back) * QBLK + jnp.arange((back + 1) * QBLK)[None]
    dist = qpos[:, :, None] - kpos[:, None, :]
    mask = (dist >= 0) & (dist <= WINDOW) & (kpos[:, None, :] >= 0)
    s = jnp.einsum('bnqhd,bnkd->bnqhk', q.reshape(B, nb, QBLK, H, D), band[..., 0, :]).astype(jnp.float32) * D ** -0.5
    s = s + jnp.swapaxes(bias_a[t5_bucket(dist)], -1, -2).astype(jnp.float32)[None]
    p = masked_softmax(s, mask[None, :, :, None, :])
    o = jnp.einsum('bnqhk,bnkd->bnqhd', p.astype(q.dtype), band[..., 1, :])
    return o.reshape(B, S, H, D)


def nsa_combine(gate, o_cmp, o_slc, o_win):
    return gate[..., 0:1] * o_cmp + gate[..., 1:2] * o_slc + gate[..., 2:3] * o_win


def mla_project(pr, pos, lp):
    cq = rms_norm(pr['mla_dq'], lp['g_qlat'])
    q = rms_norm(jnp.einsum('btc,chd->bthd', cq, lp['w_uq']), lp['g_q'])
    g_k = lp['g_k']
    q_pe = rope(q[..., MLA_NOPE:], pos)
    q_lat = jnp.einsum('bthd,chd->bthc', q[..., :MLA_NOPE] * g_k[:MLA_NOPE], lp['w_uk'])
    ckv = rms_norm(pr['mla_dkv'], lp['g_kvlat'])
    kpe = pr['mla_kpe']
    k_nope = jnp.einsum('btc,chd->bthd', ckv, lp['w_uk']).astype(jnp.float32)
    kpe_f = kpe.astype(jnp.float32)
    ms = (jnp.sum(k_nope * k_nope, -1) + jnp.sum(kpe_f * kpe_f, -1, keepdims=True)) / (MLA_NOPE + MLA_ROPE)
    inv_r = lax.rsqrt(ms + EPS).astype(ckv.dtype)
    k_pe = rope(kpe * g_k[MLA_NOPE:], pos)
    return q_lat, q_pe, jnp.concatenate([ckv, k_pe, inv_r], axis=-1)


def mla_attend(qpos, q_lat, q_pe, rows, w_uv):
    c = rows[..., :KV_LORA]
    kpe = rows[..., KV_LORA:KV_LORA + MLA_ROPE]
    inv_r = rows[..., KV_LORA + MLA_ROPE:]
    s = (jnp.einsum('bthc,bnc->bthn', q_lat, c) + jnp.einsum('bthr,bnr->bthn', q_pe, kpe)).astype(jnp.float32)
    s = s * jnp.swapaxes(inv_r, 1, 2).astype(jnp.float32)[:, None] * (MLA_NOPE + MLA_ROPE) ** -0.5
    mask = qpos[:, None] >= jnp.arange(rows.shape[1])[None]
    p = masked_softmax(s, mask[None, :, None, :])
    o_lat = jnp.einsum('bthn,bnc->bthc', p.astype(c.dtype), c)
    return jnp.einsum('bthc,chd->bthd', o_lat, w_uv)


def dsa_attend(qpos, q, q_idx, w_idx, k_idx, get_rows, bias_c):
    L = k_idx.shape[1]
    ksel = min(IDX_TOPK, L // 4)
    r = jax.nn.relu(jnp.einsum('bthd,bsd->bths', q_idx, k_idx).astype(jnp.float32))
    score = jnp.einsum('bths,bth->bts', r, w_idx.astype(jnp.float32))
    score = jnp.where((qpos[:, None] >= jnp.arange(L)[None])[None], score, -jnp.inf)
    _, idx = lax.top_k(score, ksel)
    return attend_gathered(q, get_rows(idx), idx, qpos, bias_c)


def project(h, pos, lp):
    B, T = h.shape[:2]
    pr = split_proj(h @ lp['w_in'])
    g_nsa, g_dsa = lp['g_nsa_qk'], lp['g_dsa_qk']
    q_lat, q_pe, mla_row = mla_project(pr, pos, lp)
    return dict(
        q_a=rms_norm(pr['nsa_q'].reshape(B, T, NSA_HEADS, NSA_HD), g_nsa[0]),
        kv_cmp=pr['nsa_cmp_kv'].reshape(B, T, 2, NSA_HD),
        kv_slc=norm_key(pr['nsa_slc_kv'].reshape(B, T, 2, NSA_HD), g_nsa[2]),
        kv_win=norm_key(pr['nsa_win_kv'].reshape(B, T, 2, NSA_HD), g_nsa[3]),
        gate=jax.nn.sigmoid(pr['nsa_gate'].reshape(B, T, NSA_HEADS, 3)),
        q_lat=q_lat, q_pe=q_pe, mla_row=mla_row,
        q_c=rms_norm(pr['dsa_q'].reshape(B, T, DSA_HEADS, DSA_HD), g_dsa[0]),
        kv_c=norm_key(pr['dsa_kv'].reshape(B, T, 2, DSA_HD), g_dsa[1]),
        q_idx=pr['dsa_qidx'].reshape(B, T, IDX_HEADS, IDX_DIM) * IDX_DIM ** -0.5,
        k_idx=pr['dsa_kidx'],
        w_idx=pr['dsa_w'] * IDX_HEADS ** -0.5,
    )


def mix_out_and_ffn(x, o_a, o_b, o_c, lp):
    B, T = x.shape[:2]
    mixed = jnp.concatenate([o_a.reshape(B, T, -1), o_b.reshape(B, T, -1), o_c.reshape(B, T, -1)], axis=-1)
    x = x + mixed @ lp['w_out']
    h = rms_norm(x, lp['g_ffn'])
    return x + jnp.square(jax.nn.relu(h @ lp['w_up'])) @ lp['w_down']


def layer_prompt(x, lp, rel_bias):
    B, S, _ = x.shape
    pos = jnp.arange(S)
    pc = project(rms_norm(x, lp['g_mix']), pos, lp)
    bias_a, bias_c = rel_bias[:, :NSA_HEADS], rel_bias[:, NSA_HEADS:]
    bidx = jnp.arange(B)[:, None, None]
    q_a, kv_slc, kv_c, k_idx, mla_row, w_uv = pc['q_a'], pc['kv_slc'], pc['kv_c'], pc['k_idx'], pc['mla_row'], lp['w_uv']
    o_cmp, idx = nsa_cmp_branch(q_a, pos, nsa_compress(pc['kv_cmp'], lp['w_cmp']), lp['g_nsa_qk'][1], bias_a, S)
    o_slc = map_query_blocks(
        lambda qp, q, ix: nsa_slc_branch(qp, q, ix, lambda tok: kv_slc[bidx, tok], bias_a), pos, q_a, idx)
    o_win = nsa_win_prompt(q_a, pc['kv_win'], bias_a)
    o_a = nsa_combine(pc['gate'], o_cmp, o_slc, o_win)
    o_b = map_query_blocks(lambda qp, ql, qr: mla_attend(qp, ql, qr, mla_row, w_uv), pos, pc['q_lat'], pc['q_pe'])
    o_c = map_query_blocks(
        lambda qp, q, qi, wi: dsa_attend(qp, q, qi, wi, k_idx, lambda ix: kv_c[bidx, ix], bias_c),
        pos, pc['q_c'], pc['q_idx'], pc['w_idx'])
    y = mix_out_and_ffn(x, o_a, o_b, o_c, lp)
    new = (pc['kv_cmp'], kv_slc, pc['kv_win'][:, -min(WINDOW, S):], mla_row, kv_c, k_idx)
    return y, new


def layer_sample(x, lp, rel_bias, page_table, c_cmp, c_slc, s_win, c_mla, c_dkv, c_kidx):
    T = x.shape[1]
    past = page_table.shape[1] * PAGE_SIZE
    L = past + T
    pos = past + jnp.arange(T)
    pc = project(rms_norm(x, lp['g_mix']), pos, lp)
    bias_a, bias_c = rel_bias[:, :NSA_HEADS], rel_bias[:, NSA_HEADS:]
    q_a = pc['q_a']
    cmp_rows = jnp.concatenate([gather_past(c_cmp, page_table), pc['kv_cmp']], axis=1)
    o_cmp, idx = nsa_cmp_branch(q_a, pos, nsa_compress(cmp_rows, lp['w_cmp']), lp['g_nsa_qk'][1], bias_a, L)
    o_slc = nsa_slc_branch(pos, q_a, idx,
                           lambda tok: gather_rows_paged(c_slc, page_table, pc['kv_slc'], tok), bias_a)
    win_kv = jnp.concatenate([s_win, pc['kv_win']], axis=1)
    win_pos = past - s_win.shape[1] + jnp.arange(win_kv.shape[1])
    dist = pos[:, None] - win_pos[None]
    o_win, _ = attend_shared(q_a, win_kv, dist, (dist >= 0) & (dist <= WINDOW), bias_a)
    o_a = nsa_combine(pc['gate'], o_cmp, o_slc, o_win)
    mla_rows = jnp.concatenate([gather_past(c_mla, page_table), pc['mla_row']], axis=1)
    o_b = mla_attend(pos, pc['q_lat'], pc['q_pe'], mla_rows, lp['w_uv'])
    kidx_all = jnp.concatenate([gather_past(c_kidx, page_table), pc['k_idx']], axis=1)
    o_c = dsa_attend(pos, pc['q_c'], pc['q_idx'], pc['w_idx'], kidx_all,
                     lambda ix: gather_rows_paged(c_dkv, page_table, pc['kv_c'], ix), bias_c)
    y = mix_out_and_ffn(x, o_a, o_b, o_c, lp)
    new = (pc['kv_cmp'], pc['kv_slc'], win_kv[:, -min(WINDOW, L):], pc['mla_row'], pc['kv_c'], pc['k_idx'])
    return y, new


def setup_inputs(seed: int = 0) -> dict:
    key = jax.random.key(seed)
    ks = iter(jax.random.split(key, 40))
    f32 = jnp.float32

    def nrm(shape, scale=1.0):
        return jax.random.normal(next(ks), shape, f32) * scale

    def gain(shape):
        return 1.0 + 0.05 * jax.random.normal(next(ks), shape, f32)

    n_pages = PAST_LEN // PAGE_SIZE
    n_used = DEC_BATCH * n_pages
    n_pool = n_used + n_used // 4
    w_buf = min(WINDOW, PAST_LEN)
    page_table = jax.random.permutation(next(ks), n_pool)[:n_used].reshape(DEC_BATCH, n_pages).astype(jnp.int32)
    mla_lat = nrm((DEPTH, n_pool, PAGE_SIZE, KV_LORA + MLA_ROPE))
    mla_inv = jax.random.uniform(next(ks), (DEPTH, n_pool, PAGE_SIZE, MLA_HEADS), f32, 0.8, 1.2)
    return {
        'x_prompt': nrm((BATCH, SEQ, D_MODEL)),
        'x_sample': nrm((DEC_BATCH, DEC_SEQ, D_MODEL)),
        'cache_nsa_cmp_kv': nrm((DEPTH, n_pool, PAGE_SIZE, 2, NSA_HD)),
        'cache_nsa_slc_kv': nrm((DEPTH, n_pool, PAGE_SIZE, 2, NSA_HD)),
        'state_nsa_win_kv': nrm((DEPTH, DEC_BATCH, w_buf, 2, NSA_HD)),
        'cache_mla': jnp.concatenate([mla_lat, mla_inv], axis=-1),
        'cache_dsa_kv': nrm((DEPTH, n_pool, PAGE_SIZE, 2, DSA_HD)),
        'cache_dsa_kidx': nrm((DEPTH, n_pool, PAGE_SIZE, IDX_DIM)),
        'page_table': page_table,
        'rel_bias': nrm((NUM_BUCKETS, NSA_HEADS + DSA_HEADS), 0.5),
        'g_mix': gain((DEPTH, D_MODEL)),
        'w_in': nrm((DEPTH, D_MODEL, PROJ_WIDTH), D_MODEL ** -0.5),
        'g_nsa_qk': gain((DEPTH, 4, NSA_HD)),
        'w_nsa_cmp': (1.0 + 0.1 * nrm((DEPTH, 2, CMP_LEN))) * CMP_LEN ** -0.5,
        'g_mla_qlat': gain((DEPTH, Q_LORA)),
        'g_mla_kvlat': gain((DEPTH, KV_LORA)),
        'w_mla_uq': nrm((DEPTH, Q_LORA, MLA_HEADS, MLA_NOPE + MLA_ROPE), Q_LORA ** -0.5),
        'w_mla_uk': nrm((DEPTH, KV_LORA, MLA_HEADS, MLA_NOPE), KV_LORA ** -0.5),
        'w_mla_uv': nrm((DEPTH, KV_LORA, MLA_HEADS, MLA_VD), KV_LORA ** -0.5),
        'g_mla_q': gain((DEPTH, MLA_NOPE + MLA_ROPE)),
        'g_mla_k': gain((DEPTH, MLA_NOPE + MLA_ROPE)),
        'g_dsa_qk': gain((DEPTH, 2, DSA_HD)),
        'w_out': nrm((DEPTH, MIX_WIDTH, D_MODEL), MIX_WIDTH ** -0.5),
        'g_ffn': gain((DEPTH, D_MODEL)),
        'w_up': nrm((DEPTH, D_MODEL, D_FF), D_MODEL ** -0.5),
        'w_down': nrm((DEPTH, D_FF, D_MODEL), D_FF ** -0.5),
    }


def reference(x_prompt, x_sample, cache_nsa_cmp_kv, cache_nsa_slc_kv, state_nsa_win_kv, cache_mla,
              cache_dsa_kv, cache_dsa_kidx, page_table, rel_bias, g_mix, w_in, g_nsa_qk, w_nsa_cmp,
              g_mla_qlat, g_mla_kvlat, w_mla_uq, w_mla_uk, w_mla_uv, g_mla_q, g_mla_k, g_dsa_qk,
              w_out, g_ffn, w_up, w_down):
    yp, ys = x_prompt, x_sample
    prompt_new, sample_new = [], []
    for l in range(DEPTH):
        lp = dict(g_mix=g_mix[l], w_in=w_in[l], g_nsa_qk=g_nsa_qk[l], w_cmp=w_nsa_cmp[l],
                  g_qlat=g_mla_qlat[l], g_kvlat=g_mla_kvlat[l], w_uq=w_mla_uq[l], w_uk=w_mla_uk[l],
                  w_uv=w_mla_uv[l], g_q=g_mla_q[l], g_k=g_mla_k[l], g_dsa_qk=g_dsa_qk[l],
                  w_out=w_out[l], g_ffn=g_ffn[l], w_up=w_up[l], w_down=w_down[l])
        yp, newp = layer_prompt(yp, lp, rel_bias)
        ys, news = layer_sample(ys, lp, rel_bias, page_table, cache_nsa_cmp_kv[l], cache_nsa_slc_kv[l],
                                state_nsa_win_kv[l], cache_mla[l], cache_dsa_kv[l], cache_dsa_kidx[l])
        prompt_new.append(newp)
        sample_new.append(news)
    p_cmp, p_slc, p_win, p_mla, p_dkv, p_kidx = [jnp.stack(a) for a in zip(*prompt_new)]
    s_cmp, s_slc, s_win, s_mla, s_dkv, s_kidx = [jnp.stack(a) for a in zip(*sample_new)]
    return (yp, ys, p_cmp, p_slc, p_win, p_mla, p_dkv, p_kidx, s_cmp, s_slc, s_win, s_mla, s_dkv, s_kidx)
```

```python
import functools
import math

import numpy as np
import jax
import jax.numpy as jnp
from jax import lax
from jax.experimental import pallas as pl
from jax.experimental.pallas import tpu as pltpu

F32 = jnp.float32
BF16 = jnp.bfloat16

D_MODEL = 2048
PAGE = 128
NSA_HEADS = 4
HD = 128
CMP_LEN = 32
CMP_STRIDE = 16
SEL_BLOCK = 64
N_SEL = 16
WINDOW = 512
MLA_HEADS = 8
MLA_NOPE = 128
MLA_ROPE = 64
MLA_QK = MLA_NOPE + MLA_ROPE
Q_LORA = 512
KV_LORA = 256
MLA_ROW = KV_LORA + MLA_ROPE + MLA_HEADS
DSA_HEADS = 4
IDX_HEADS = 16
IDX_DIM = 64
IDX_TOPK = 256
NUM_BUCKETS = 32
MAX_DISTANCE = 128
ROPE_THETA = 10000.0
EPS = 1e-6
NEG = -1e30
FORCE_BONUS = 1e4

LANE = 128
QT = 128
TP = 8
PG = 8
ROWP = 384
MM_T = 512
MM_TK = 2048
POST_T = 256
VMEM_LIMIT = 56 * 1024 * 1024
BIAS_CLIP = 255

C_QA, C_CMP, C_SLC, C_WIN, C_QC, C_KVC, C_QIDX, C_DQ, C_DKV, C_A, C_B, PROJ_P = (
    0, 512, 768, 1024, 1280, 1792, 2048, 3072, 3584, 3840, 3968, 4096)
GATE_W = 3 * NSA_HEADS


def _nt(a, b):
    return lax.dot_general(a, b, (((1,), (1,)), ((), ())), preferred_element_type=F32)


def _mm(a, b):
    return jnp.dot(a, b, preferred_element_type=F32)


def _cparams(sem):
    return pltpu.CompilerParams(dimension_semantics=sem, vmem_limit_bytes=VMEM_LIMIT)


def _softmax_step(m, l, acc, s, mask, v):
    if mask is not None:
        s = jnp.where(mask, s, NEG)
    m_new = jnp.maximum(m, jnp.max(s, axis=-1, keepdims=True))
    alpha = jnp.exp(m - m_new)
    e = jnp.exp(s - m_new)
    if mask is not None:
        e = jnp.where(mask, e, 0.0)
    l = alpha * l + jnp.sum(e, axis=-1, keepdims=True)
    acc = alpha * acc + _mm(e.astype(BF16), v)
    return m_new, l, acc


def _finish(l, acc):
    return acc / jnp.maximum(l, 1e-30)


def _top_blocks(score, n_pick):
    lane = lax.broadcasted_iota(jnp.int32, score.shape, 1)
    width = score.shape[1]
    sel = jnp.zeros(score.shape, F32)
    for _ in range(n_pick):
        mx = jnp.max(score, axis=-1, keepdims=True)
        idx = jnp.min(jnp.where(score == mx, lane, width), axis=-1, keepdims=True)
        hit = lane == idx
        sel = jnp.where(hit, 1.0, sel)
        score = jnp.where(hit, -jnp.inf, score)
    return sel


def _order_key(x):
    bits = lax.bitcast_convert_type(x, jnp.int32)
    return jnp.where(bits < 0, bits ^ jnp.int32(0x7FFFFFFF), bits)


def _kth_threshold(count_ge, rows, k):
    def step(i, t):
        cand = t + lax.shift_left(jnp.int32(1), jnp.int32(31) - i)
        return jnp.where(count_ge(cand) >= k, cand, t)
    t0 = jnp.full((rows, 1), jnp.iinfo(jnp.int32).min, jnp.int32)
    return lax.fori_loop(0, 32, step, t0)


def _mm_full_kernel(*refs, has_gain, has_res, act):
    a_ref, w_ref = refs[0], refs[1]
    k = 2
    g_ref = r_ref = None
    if has_gain:
        g_ref = refs[k]; k += 1
    if has_res:
        r_ref = refs[k]; k += 1
    o_ref, abuf = refs[k], refs[k + 1]

    @pl.when(pl.program_id(1) == 0)
    def _():
        a = a_ref[...].astype(F32)
        if has_gain:
            ms = jnp.mean(a * a, axis=-1, keepdims=True)
            a = a * lax.rsqrt(ms + EPS) * g_ref[...]
        abuf[...] = a.astype(BF16)

    y = _mm(abuf[...], w_ref[...])
    if act == "relu2":
        y = jnp.square(jnp.maximum(y, 0.0))
    if has_res:
        y = y + r_ref[...]
    o_ref[...] = y.astype(o_ref.dtype)


def _mm_full(a, w, gain=None, res=None, act=None, out_dtype=F32):
    m, kdim = a.shape
    n = w.shape[1]
    ins = [a, w]
    specs = [pl.BlockSpec((MM_T, kdim), lambda i, j: (i, 0)),
             pl.BlockSpec((kdim, MM_T), lambda i, j: (0, j))]
    if gain is not None:
        ins.append(gain.reshape(1, kdim).astype(F32))
        specs.append(pl.BlockSpec((1, kdim), lambda i, j: (0, 0)))
    if res is not None:
        ins.append(res)
        specs.append(pl.BlockSpec((MM_T, MM_T), lambda i, j: (i, j)))
    return pl.pallas_call(
        functools.partial(_mm_full_kernel, has_gain=gain is not None, has_res=res is not None, act=act),
        grid=(m // MM_T, n // MM_T),
        in_specs=specs,
        out_specs=pl.BlockSpec((MM_T, MM_T), lambda i, j: (i, j)),
        out_shape=jax.ShapeDtypeStruct((m, n), out_dtype),
        scratch_shapes=[pltpu.VMEM((MM_T, kdim), BF16)],
        compiler_params=_cparams(("parallel", "arbitrary")),
    )(*ins)


def _mm_kt_kernel(a_ref, w_ref, r_ref, o_ref, acc):
    kk = pl.program_id(2)

    @pl.when(kk == 0)
    def _():
        acc[...] = r_ref[...]

    acc[...] += _mm(a_ref[...], w_ref[...])

    @pl.when(kk == pl.num_programs(2) - 1)
    def _():
        o_ref[...] = acc[...]


def _mm_ktiled(a, w, res):
    m, kdim = a.shape
    n = w.shape[1]
    return pl.pallas_call(
        _mm_kt_kernel,
        grid=(m // MM_T, n // MM_T, kdim // MM_TK),
        in_specs=[pl.BlockSpec((MM_T, MM_TK), lambda i, j, k: (i, k)),
                  pl.BlockSpec((MM_TK, MM_T), lambda i, j, k: (k, j)),
                  pl.BlockSpec((MM_T, MM_T), lambda i, j, k: (i, j))],
        out_specs=pl.BlockSpec((MM_T, MM_T), lambda i, j, k: (i, j)),
        out_shape=jax.ShapeDtypeStruct((m, n), F32),
        scratch_shapes=[pltpu.VMEM((MM_T, MM_T), F32)],
        compiler_params=_cparams(("parallel", "parallel", "arbitrary")),
    )(a, w, res)


def _post_kernel(p_ref, cs_ref, g_ref, wuq_ref, wukt_ref, wukf_ref,
                 qa_ref, slc_ref, win_ref, kvc_ref, qc_ref, qidx_ref, small_ref, qmla_ref, row_ref):
    def nrm(x, g):
        ms = jnp.mean(x * x, axis=-1, keepdims=True)
        return x * lax.rsqrt(ms + EPS) * g

    def gain(r, w):
        return g_ref[r:r + 1, 0:w]

    for h in range(NSA_HEADS):
        qa_ref[:, HD * h:HD * (h + 1)] = nrm(p_ref[:, C_QA + HD * h:C_QA + HD * (h + 1)], gain(0, HD)).astype(BF16)
        qc_ref[:, HD * h:HD * (h + 1)] = nrm(p_ref[:, C_QC + HD * h:C_QC + HD * (h + 1)], gain(3, HD)).astype(BF16)
    for ref, col, r in ((slc_ref, C_SLC, 1), (win_ref, C_WIN, 2), (kvc_ref, C_KVC, 4)):
        ref[:, 0:HD] = nrm(p_ref[:, col:col + HD], gain(r, HD))
        ref[:, HD:2 * HD] = p_ref[:, col + HD:col + 2 * HD]
    qidx_ref[...] = (p_ref[:, C_QIDX:C_QIDX + IDX_HEADS * IDX_DIM] * IDX_DIM ** -0.5).astype(BF16)

    tm = p_ref.shape[0]
    lane = lax.broadcasted_iota(jnp.int32, (tm, LANE), 1)
    blk_b = p_ref[:, C_B:C_B + LANE]
    small_ref[...] = jnp.where(lane < GATE_W, jax.nn.sigmoid(blk_b), blk_b * IDX_HEADS ** -0.5)

    cos = cs_ref[:, 0:LANE]
    sin = cs_ref[:, LANE:2 * LANE]

    def rope(x):
        swapped = jnp.where(lane < MLA_ROPE // 2, pltpu.roll(x, LANE - MLA_ROPE // 2, 1),
                            pltpu.roll(x, MLA_ROPE // 2, 1))
        return x * cos + swapped * sin

    cq = nrm(p_ref[:, C_DQ:C_DQ + Q_LORA], gain(5, Q_LORA)).astype(BF16)
    q2 = _mm(cq, wuq_ref[...])
    for h in range(MLA_HEADS):
        blk = q2[:, 2 * LANE * h:2 * LANE * (h + 1)]
        ms = jnp.sum(blk * blk, axis=-1, keepdims=True) * (1.0 / MLA_QK)
        qn = blk * lax.rsqrt(ms + EPS) * gain(7, 2 * LANE)
        qlat = _mm((qn[:, 0:MLA_NOPE] * gain(8, MLA_NOPE)).astype(BF16), wukt_ref[h])
        qmla_ref[:, ROWP * h:ROWP * h + KV_LORA] = qlat.astype(BF16)
        qmla_ref[:, ROWP * h + KV_LORA:ROWP * (h + 1)] = rope(qn[:, MLA_NOPE:2 * LANE]).astype(BF16)

    ckv = nrm(p_ref[:, C_DKV:C_DKV + KV_LORA], gain(6, KV_LORA))
    kn = _mm(ckv.astype(BF16), wukf_ref[...])
    kpe = jnp.where(lane < MLA_ROPE, p_ref[:, C_A:C_A + LANE], 0.0)
    ssq_pe = jnp.sum(kpe * kpe, axis=-1, keepdims=True)
    third = rope(kpe * gain(9, LANE))
    for h in range(MLA_HEADS):
        kh = kn[:, MLA_NOPE * h:MLA_NOPE * (h + 1)]
        ms = (jnp.sum(kh * kh, axis=-1, keepdims=True) + ssq_pe) * (1.0 / MLA_QK)
        third = jnp.where(lane == MLA_ROPE + h, lax.rsqrt(ms + EPS), third)
    row_ref[:, 0:KV_LORA] = ckv
    row_ref[:, KV_LORA:ROWP] = third


def _post(proj, cs, gpack, wuq2, wukt, wukf):
    m = proj.shape[0]
    row = lambda w: pl.BlockSpec((POST_T, w), lambda i: (i, 0))
    full = lambda a: pl.BlockSpec(a.shape, lambda i: (0,) * a.ndim)
    widths = (4 * HD, 2 * HD, 2 * HD, 2 * HD, 4 * HD, IDX_HEADS * IDX_DIM, LANE, MLA_HEADS * ROWP, ROWP)
    dtypes = (BF16, F32, F32, F32, BF16, BF16, F32, BF16, F32)
    return pl.pallas_call(
        _post_kernel,
        grid=(m // POST_T,),
        in_specs=[row(PROJ_P), row(2 * LANE), full(gpack), full(wuq2), full(wukt), full(wukf)],
        out_specs=[row(w) for w in widths],
        out_shape=[jax.ShapeDtypeStruct((m, w), d) for w, d in zip(widths, dtypes)],
        compiler_params=_cparams(("parallel",)),
    )(proj, cs, gpack, wuq2, wukt, wukf)


def _compress_kernel(kv_ref, wk_ref, wv_ref, g_ref, ck_ref, cv_ref):
    ck = _mm(wk_ref[...], kv_ref[:, 0:HD].astype(BF16))
    ms = jnp.mean(ck * ck, axis=-1, keepdims=True)
    ck_ref[...] = (ck * lax.rsqrt(ms + EPS) * g_ref[...]).astype(BF16)
    cv_ref[...] = _mm(wv_ref[...], kv_ref[:, HD:2 * HD].astype(BF16)).astype(BF16)


def _compress_prompt(kvcmp, wk, wv, g, nb, s):
    ncp = wk.shape[0]
    return pl.pallas_call(
        _compress_kernel,
        grid=(nb,),
        in_specs=[pl.BlockSpec((s, 2 * HD), lambda b: (b, 0)),
                  pl.BlockSpec(wk.shape, lambda b: (0, 0)),
                  pl.BlockSpec(wv.shape, lambda b: (0, 0)),
                  pl.BlockSpec((1, HD), lambda b: (0, 0))],
        out_specs=[pl.BlockSpec((None, ncp, HD), lambda b: (b, 0, 0))] * 2,
        out_shape=[jax.ShapeDtypeStruct((nb, ncp, HD), BF16)] * 2,
        compiler_params=_cparams(("parallel",)),
    )(kvcmp, wk, wv, g)


def _tile_dist(qi, kc, rows):
    t = lax.broadcasted_iota(jnp.int32, (rows, QT), 0) % QT
    c = lax.broadcasted_iota(jnp.int32, (rows, QT), 1)
    return QT * (qi - kc) + t - c


def _nsa_prompt_kernel(qa_ref, small_ref, ck_ref, cv_ref, slc_ref, win_ref, tb_ref, cb_ref, cover_ref,
                       e_ref, o_ref, *, n_cmp, n_pick):
    qi = pl.program_id(1)
    rows = NSA_HEADS * QT
    scale = HD ** -0.5
    q4 = jnp.concatenate([qa_ref[:, HD * h:HD * (h + 1)] for h in range(NSA_HEADS)], axis=0)

    ncp = ck_ref.shape[0]
    s = _nt(q4, ck_ref[...]) * scale + cb_ref[...]
    t = lax.broadcasted_iota(jnp.int32, (rows, ncp), 0) % QT
    ci = lax.broadcasted_iota(jnp.int32, (rows, ncp), 1)
    mask = (QT * qi + t - CMP_STRIDE * ci - (CMP_LEN - 1) >= 0) & (ci < n_cmp)
    s = jnp.where(mask, s, NEG)
    e = jnp.where(mask, jnp.exp(s - jnp.max(s, axis=-1, keepdims=True)), 0.0)
    p = e / jnp.maximum(jnp.sum(e, axis=-1, keepdims=True), 1e-30)
    o_cmp = _mm(p.astype(BF16), cv_ref[...])
    psum = p[0:QT] + p[QT:2 * QT] + p[2 * QT:3 * QT] + p[3 * QT:4 * QT]
    imp = jnp.dot(psum, cover_ref[...], preferred_element_type=F32, precision=lax.Precision.HIGHEST)
    ns = imp.shape[1]
    blk = lax.broadcasted_iota(jnp.int32, (QT, ns), 1)
    qb = (QT * qi + lax.broadcasted_iota(jnp.int32, (QT, ns), 0)) // SEL_BLOCK
    forced = (blk == 0) | (blk == qb) | (blk == qb - 1)
    score = jnp.where(blk <= qb, imp + jnp.where(forced, FORCE_BONUS, 0.0), -jnp.inf)
    sel = _top_blocks(score, n_pick).astype(BF16)

    init = (jnp.full((rows, 1), NEG, F32), jnp.zeros((rows, 1), F32), jnp.zeros((rows, HD), F32))

    def chunk(ref, kc):
        kv = ref[pl.ds(pl.multiple_of(kc * QT, QT), QT), :]
        return kv[:, 0:HD].astype(BF16), kv[:, HD:2 * HD].astype(BF16)

    def slc_body(kc, carry):
        k, v = chunk(slc_ref, kc)
        sc = _nt(q4, k) * scale + tb_ref[jnp.minimum(qi - kc, 2)]
        picked = _mm(sel, e_ref[kc])
        picked = jnp.concatenate([picked] * NSA_HEADS, axis=0)
        m_ = (picked > 0.5) & (_tile_dist(qi, kc, rows) >= 0)
        return _softmax_step(*carry, sc, m_, v)

    _, l, acc = lax.fori_loop(0, qi + 1, slc_body, init)
    o_slc = _finish(l, acc)

    def win_body(kc, carry):
        k, v = chunk(win_ref, kc)
        sc = _nt(q4, k) * scale + tb_ref[jnp.minimum(qi - kc, 2)]
        dist = _tile_dist(qi, kc, rows)
        return _softmax_step(*carry, sc, (dist >= 0) & (dist <= WINDOW), v)

    _, l, acc = lax.fori_loop(jnp.maximum(qi - WINDOW // QT, 0), qi + 1, win_body, init)
    o_win = _finish(l, acc)

    gate = small_ref[...]
    for h in range(NSA_HEADS):
        r = slice(QT * h, QT * (h + 1))
        o_ref[:, HD * h:HD * (h + 1)] = (gate[:, 3 * h:3 * h + 1] * o_cmp[r] + gate[:, 3 * h + 1:3 * h + 2] * o_slc[r]
                                         + gate[:, 3 * h + 2:3 * h + 3] * o_win[r])


def _nsa_prompt(qa, small, ck, cv, slc, win, tb, cb, cover, etab, nb, s):
    nq = s // QT
    ncp = ck.shape[1]
    rows = NSA_HEADS * QT
    tile = lambda w: pl.BlockSpec((QT, w), lambda b, i: (b * nq + i, 0))
    seq = lambda w: pl.BlockSpec((s, w), lambda b, i: (b, 0))
    const = lambda a: pl.BlockSpec(a.shape, lambda b, i: (0,) * a.ndim)
    return pl.pallas_call(
        functools.partial(_nsa_prompt_kernel, n_cmp=s // CMP_STRIDE - 1, n_pick=min(N_SEL, s // SEL_BLOCK)),
        grid=(nb, nq),
        in_specs=[tile(4 * HD), tile(LANE),
                  pl.BlockSpec((None, ncp, HD), lambda b, i: (b, 0, 0)),
                  pl.BlockSpec((None, ncp, HD), lambda b, i: (b, 0, 0)),
                  seq(2 * HD), seq(2 * HD), const(tb),
                  pl.BlockSpec((None, rows, ncp), lambda b, i: (i, 0, 0)),
                  const(cover), const(etab)],
        out_specs=pl.BlockSpec((QT, 4 * HD), lambda b, i: (b * nq + i, 0)),
        out_shape=jax.ShapeDtypeStruct((nb * s, 4 * HD), F32),
        compiler_params=_cparams(("parallel", "arbitrary")),
    )(qa, small, ck, cv, slc, win, tb, cb, cover, etab)


def _mla_prompt_kernel(q_ref, row_ref, inv_ref, wuv_ref, o_ref, m_s, l_s, acc_s):
    qi = pl.program_id(1)
    rows = MLA_HEADS * QT
    scale = MLA_QK ** -0.5
    q8 = jnp.concatenate([q_ref[:, ROWP * h:ROWP * (h + 1)] for h in range(MLA_HEADS)], axis=0)
    m_s[...] = jnp.full((rows, 1), NEG, F32)
    l_s[...] = jnp.zeros((rows, 1), F32)
    acc_s[...] = jnp.zeros((rows, KV_LORA), F32)

    def body(kc, _):
        kch = row_ref[pl.ds(pl.multiple_of(kc * QT, QT), QT), :].astype(BF16)
        inv = inv_ref[kc]
        mult = jnp.concatenate([jnp.broadcast_to(inv[h:h + 1], (QT, QT)) for h in range(MLA_HEADS)], axis=0)
        sc = _nt(q8, kch) * (mult * scale)
        m_, l_, a_ = _softmax_step(m_s[...], l_s[...], acc_s[...], sc, _tile_dist(qi, kc, rows) >= 0,
                                   kch[:, 0:KV_LORA])
        m_s[...] = m_
        l_s[...] = l_
        acc_s[...] = a_
        return 0

    lax.fori_loop(0, qi + 1, body, 0)
    o_lat = _finish(l_s[...], acc_s[...]).astype(BF16)
    for h in range(MLA_HEADS):
        o_ref[:, HD * h:HD * (h + 1)] = _mm(o_lat[QT * h:QT * (h + 1)], wuv_ref[h])


def _mla_prompt(qmla, rowp, invt, wuv, nb, s):
    nq = s // QT
    rows = MLA_HEADS * QT
    return pl.pallas_call(
        _mla_prompt_kernel,
        grid=(nb, nq),
        in_specs=[pl.BlockSpec((QT, MLA_HEADS * ROWP), lambda b, i: (b * nq + i, 0)),
                  pl.BlockSpec((s, ROWP), lambda b, i: (b, 0)),
                  pl.BlockSpec((None, nq, MLA_HEADS, QT), lambda b, i: (b, 0, 0, 0)),
                  pl.BlockSpec(wuv.shape, lambda b, i: (0, 0, 0))],
        out_specs=pl.BlockSpec((QT, MLA_HEADS * HD), lambda b, i: (b * nq + i, 0)),
        out_shape=jax.ShapeDtypeStruct((nb * s, MLA_HEADS * HD), F32),
        scratch_shapes=[pltpu.VMEM((rows, 1), F32), pltpu.VMEM((rows, 1), F32), pltpu.VMEM((rows, KV_LORA), F32)],
        compiler_params=_cparams(("parallel", "arbitrary")),
    )(qmla, rowp, invt, wuv)


def _dsa_prompt_kernel(qc_ref, qidx_ref, small_ref, kidx_ref, kvc_ref, tb_ref, o_ref, key_s, *, ksel):
    qi = pl.program_id(1)
    rows = DSA_HEADS * QT
    scale = HD ** -0.5
    lane = lax.broadcasted_iota(jnp.int32, (QT, LANE), 1)
    w = small_ref[...]

    halves = []
    for j in range(IDX_HEADS // 2):
        x = qidx_ref[:, LANE * j:LANE * (j + 1)]
        halves.append((jnp.where(lane < IDX_DIM, x, 0).astype(BF16), jnp.where(lane >= IDX_DIM, x, 0).astype(BF16)))

    def idx_body(kc, _):
        kk = kidx_ref[pl.ds(pl.multiple_of(kc * QT, QT), QT), :]
        acc = jnp.zeros((QT, QT), F32)
        for j, (lo, hi) in enumerate(halves):
            c = GATE_W + 2 * j
            acc = acc + jnp.maximum(_nt(lo, kk), 0.0) * w[:, c:c + 1]
            acc = acc + jnp.maximum(_nt(hi, kk), 0.0) * w[:, c + 1:c + 2]
        sc = jnp.where(_tile_dist(qi, kc, QT) >= 0, acc, -jnp.inf)
        key_s[kc] = _order_key(sc)
        return 0

    lax.fori_loop(0, qi + 1, idx_body, 0)

    def count_ge(cand):
        def body(kc, c):
            return c + (key_s[kc] >= cand).astype(jnp.int32)
        c = lax.fori_loop(0, qi + 1, body, jnp.zeros((QT, QT), jnp.int32))
        return jnp.sum(c, axis=-1, keepdims=True)

    thr = _kth_threshold(count_ge, QT, ksel)
    q4 = jnp.concatenate([qc_ref[:, HD * h:HD * (h + 1)] for h in range(DSA_HEADS)], axis=0)

    def att_body(kc, carry):
        kv = kvc_ref[pl.ds(pl.multiple_of(kc * QT, QT), QT), :]
        sc = _nt(q4, kv[:, 0:HD].astype(BF16)) * scale + tb_ref[jnp.minimum(qi - kc, 2)]
        keep = key_s[kc] >= thr
        keep = jnp.concatenate([keep] * DSA_HEADS, axis=0) & (_tile_dist(qi, kc, rows) >= 0)
        return _softmax_step(*carry, sc, keep, kv[:, HD:2 * HD].astype(BF16))

    init = (jnp.full((rows, 1), NEG, F32), jnp.zeros((rows, 1), F32), jnp.zeros((rows, HD), F32))
    _, l, acc = lax.fori_loop(0, qi + 1, att_body, init)
    o = _finish(l, acc)
    for h in range(DSA_HEADS):
        o_ref[:, HD * h:HD * (h + 1)] = o[QT * h:QT * (h + 1)]


def _dsa_prompt(qc, qidx, small, kidx2, kvc, tb, nb, s):
    nq = s // QT
    tile = lambda w: pl.BlockSpec((QT, w), lambda b, i: (b * nq + i, 0))
    return pl.pallas_call(
        functools.partial(_dsa_prompt_kernel, ksel=min(IDX_TOPK, s // 4)),
        grid=(nb, nq),
        in_specs=[tile(4 * HD), tile(IDX_HEADS * IDX_DIM), tile(LANE),
                  pl.BlockSpec((None, s, LANE), lambda b, i: (b, 0, 0)),
                  pl.BlockSpec((s, 2 * HD), lambda b, i: (b, 0)),
                  pl.BlockSpec(tb.shape, lambda b, i: (0, 0, 0))],
        out_specs=pl.BlockSpec((QT, 4 * HD), lambda b, i: (b * nq + i, 0)),
        out_shape=jax.ShapeDtypeStruct((nb * s, 4 * HD), F32),
        scratch_shapes=[pltpu.VMEM((nq, QT, QT), jnp.int32)],
        compiler_params=_cparams(("parallel", "arbitrary")),
    )(qc, qidx, small, kidx2, kvc, tb)


def _page_specs(layer, page_shape, n_groups, first_group=0):
    def index(b, g, pt, *, j):
        gg = jnp.clip(g - first_group, 0, n_groups - 1)
        return (layer, pt[b, gg * PG + j], 0, 0)
    return [pl.BlockSpec((None, None) + tuple(page_shape), functools.partial(index, j=j)) for j in range(PG)]


def _seq_spec(shape):
    return pl.BlockSpec((None,) + tuple(shape), lambda b, g, pt: (b,) + (0,) * len(shape))


def _const_spec(a):
    return pl.BlockSpec(a.shape, lambda b, g, pt: (0,) * a.ndim)


def _new_key_mask(rows, n_new):
    t = lax.broadcasted_iota(jnp.int32, (rows, PAGE), 0) % TP
    n = lax.broadcasted_iota(jnp.int32, (rows, PAGE), 1)
    return (n <= t) & (n < n_new)


def _s_cmp_kernel(pt_ref, q_ref, *refs, n_pages, n_new, n_pick):
    pages = refs[:PG]
    (new_ref, w3k_ref, w3v_ref, wn_ref, g_ref, cb_ref, cover_ref,
     ocmp_ref, sel_ref, ck_s, cv_s, carry_s) = refs[PG:]
    g = pl.program_id(1)
    rows = NSA_HEADS * TP
    nj = ck_s.shape[0]
    past = n_pages * PAGE
    row8 = lax.broadcasted_iota(jnp.int32, (8, HD), 0)

    @pl.when(g == 0)
    def _():
        ck_s[...] = jnp.zeros(ck_s.shape, F32)
        cv_s[...] = jnp.zeros(cv_s.shape, F32)
        carry_s[...] = jnp.zeros(carry_s.shape, F32)

    carry_k = carry_s[0:1, :]
    carry_v = carry_s[1:2, :]
    for j in range(PG):
        pg = pages[j][...]
        rk = _mm(w3k_ref[...], pg[:, 0:HD].astype(BF16))
        rv = _mm(w3v_ref[...], pg[:, HD:2 * HD].astype(BF16))
        off = pl.multiple_of(g * (PG * 8) + 8 * j, 8)
        ck_s[pl.ds(off, 8), :] = rk[0:8] + jnp.where(row8 == 0, carry_k, 0.0)
        cv_s[pl.ds(off, 8), :] = rv[0:8] + jnp.where(row8 == 0, carry_v, 0.0)
        carry_k = rk[8:9]
        carry_v = rv[8:9]
    carry_s[0:1, :] = carry_k
    carry_s[1:2, :] = carry_v

    @pl.when(g == pl.num_programs(1) - 1)
    def _():
        new = new_ref[...]
        tail_k = jnp.sum(new[:, 0:HD] * wn_ref[0], axis=0, keepdims=True) + carry_k
        tail_v = jnp.sum(new[:, HD:2 * HD] * wn_ref[1], axis=0, keepdims=True) + carry_v
        last = 8 * n_pages
        ck_s[last:last + 8, :] = jnp.where(row8 == 0, tail_k, 0.0)
        cv_s[last:last + 8, :] = jnp.where(row8 == 0, tail_v, 0.0)

        ck = ck_s[...]
        ms = jnp.mean(ck * ck, axis=-1, keepdims=True)
        ckn = (ck * lax.rsqrt(ms + EPS) * g_ref[...]).astype(BF16)
        s = _nt(q_ref[...], ckn) * HD ** -0.5 + cb_ref[...]
        t = lax.broadcasted_iota(jnp.int32, (rows, nj), 0) % TP
        jj = lax.broadcasted_iota(jnp.int32, (rows, nj), 1)
        mask = (jj >= 1) & (jj <= last) & (past + t - CMP_STRIDE * (jj - 1) - (CMP_LEN - 1) >= 0)
        s = jnp.where(mask, s, NEG)
        e = jnp.where(mask, jnp.exp(s - jnp.max(s, axis=-1, keepdims=True)), 0.0)
        p = e / jnp.maximum(jnp.sum(e, axis=-1, keepdims=True), 1e-30)
        ocmp_ref[...] = _mm(p.astype(BF16), cv_s[...].astype(BF16))
        psum = p[0:TP] + p[TP:2 * TP] + p[2 * TP:3 * TP] + p[3 * TP:4 * TP]
        imp = jnp.dot(psum, cover_ref[...], preferred_element_type=F32, precision=lax.Precision.HIGHEST)
        nsp = imp.shape[1]
        blk = lax.broadcasted_iota(jnp.int32, (TP, nsp), 1)
        qb = (past + lax.broadcasted_iota(jnp.int32, (TP, nsp), 0)) // SEL_BLOCK
        forced = (blk == 0) | (blk == qb) | (blk == qb - 1)
        score = jnp.where(blk <= qb, imp + jnp.where(forced, FORCE_BONUS, 0.0), -jnp.inf)
        sel_ref[...] = _top_blocks(score, n_pick)


def _s_slc_kernel(pt_ref, q_ref, sel_ref, ocmp_ref, small_ref, *refs, n_pages, n_new):
    pages = refs[:PG]
    (nslc_ref, win_ref, nwin_ref, tb_ref, wb_ref, o_ref, m_s, l_s, acc_s) = refs[PG:]
    g = pl.program_id(1)
    rows = NSA_HEADS * TP
    scale = HD ** -0.5
    q = q_ref[...]
    sel = sel_ref[...]
    nsp = sel.shape[1]
    lane_s = lax.broadcasted_iota(jnp.int32, (TP, nsp), 1)
    lane_k = lax.broadcasted_iota(jnp.int32, (TP, PAGE), 1)

    @pl.when(g == 0)
    def _():
        m_s[...] = jnp.full(m_s.shape, NEG, F32)
        l_s[...] = jnp.zeros(l_s.shape, F32)
        acc_s[...] = jnp.zeros(acc_s.shape, F32)

    def block_col(b):
        return jnp.sum(jnp.where(lane_s == b, sel, 0.0), axis=-1, keepdims=True)

    def picked(first_block):
        half = jnp.where(lane_k < SEL_BLOCK, block_col(first_block), block_col(first_block + 1))
        return jnp.concatenate([half] * NSA_HEADS, axis=0) > 0.5

    state = (m_s[...], l_s[...], acc_s[...])
    for j in range(PG):
        pidx = g * PG + j
        kv = pages[j][...]
        sc = _nt(q, kv[:, 0:HD].astype(BF16)) * scale + tb_ref[jnp.where(pidx == n_pages - 1, 0, 2)]
        state = _softmax_step(*state, sc, picked(2 * pidx), kv[:, HD:2 * HD].astype(BF16))
    m_s[...], l_s[...], acc_s[...] = state

    @pl.when(g == pl.num_programs(1) - 1)
    def _():
        new_ok = _new_key_mask(rows, n_new)
        kv = nslc_ref[...]
        sc = _nt(q, kv[:, 0:HD].astype(BF16)) * scale + tb_ref[1]
        _, l, acc = _softmax_step(*state, sc, picked(2 * n_pages) & new_ok, kv[:, HD:2 * HD].astype(BF16))
        o_slc = _finish(l, acc)

        wbuf = win_ref.shape[0]
        kv = win_ref[...]
        sc = _nt(q, kv[:, 0:HD].astype(BF16)) * scale + wb_ref[...]
        t = lax.broadcasted_iota(jnp.int32, (rows, wbuf), 0) % TP
        n = lax.broadcasted_iota(jnp.int32, (rows, wbuf), 1)
        init = (jnp.full((rows, 1), NEG, F32), jnp.zeros((rows, 1), F32), jnp.zeros((rows, HD), F32))
        st = _softmax_step(*init, sc, wbuf + t - n <= WINDOW, kv[:, HD:2 * HD].astype(BF16))
        kv = nwin_ref[...]
        sc = _nt(q, kv[:, 0:HD].astype(BF16)) * scale + tb_ref[1]
        _, l, acc = _softmax_step(*st, sc, new_ok, kv[:, HD:2 * HD].astype(BF16))
        o_win = _finish(l, acc)

        gate = small_ref[...]
        o_cmp = ocmp_ref[...]
        for h in range(NSA_HEADS):
            r = slice(TP * h, TP * (h + 1))
            o_ref[:, HD * h:HD * (h + 1)] = (gate[:, 3 * h:3 * h + 1] * o_cmp[r] + gate[:, 3 * h + 1:3 * h + 2] * o_slc[r]
                                             + gate[:, 3 * h + 2:3 * h + 3] * o_win[r])


def _s_mla_kernel(pt_ref, q_ref, *refs, n_new):
    pages = refs[:PG]
    invs = refs[PG:2 * PG]
    (new_ref, ninv_ref, wuv_ref, o_ref, m_s, l_s, acc_s) = refs[2 * PG:]
    g = pl.program_id(1)
    rows = MLA_HEADS * TP
    scale = MLA_QK ** -0.5
    q = q_ref[...]
    q_lat = q[:, 0:KV_LORA]
    q_pe = q[:, KV_LORA:MLA_ROW]

    @pl.when(g == 0)
    def _():
        m_s[...] = jnp.full(m_s.shape, NEG, F32)
        l_s[...] = jnp.zeros(l_s.shape, F32)
        acc_s[...] = jnp.zeros(acc_s.shape, F32)

    def update(state, row_ref, inv, mask):
        c = row_ref[:, 0:KV_LORA].astype(BF16)
        tail = row_ref[:, KV_LORA:MLA_ROW].astype(BF16)
        mult = jnp.concatenate([jnp.broadcast_to(inv[h:h + 1], (TP, PAGE)) for h in range(MLA_HEADS)], axis=0)
        sc = (_nt(q_lat, c) + _nt(q_pe, tail)) * (mult * scale)
        return _softmax_step(*state, sc, mask, c)

    state = (m_s[...], l_s[...], acc_s[...])
    for j in range(PG):
        state = update(state, pages[j], invs[j][...], None)
    m_s[...], l_s[...], acc_s[...] = state

    @pl.when(g == pl.num_programs(1) - 1)
    def _():
        _, l, acc = update(state, new_ref, ninv_ref[...], _new_key_mask(rows, n_new))
        o_lat = _finish(l, acc).astype(BF16)
        for h in range(MLA_HEADS):
            o_ref[:, HD * h:HD * (h + 1)] = _mm(o_lat[TP * h:TP * (h + 1)], wuv_ref[h])


def _s_dsa_kernel(pt_ref, qidx_ref, wb_ref, qc_ref, *refs, n_pages, n_new, ksel):
    kpages = refs[:PG]
    vpages = refs[PG:2 * PG]
    (nkidx_ref, nkv_ref, tb_ref, o_ref, key_s, thr_s, m_s, l_s, acc_s) = refs[2 * PG:]
    g = pl.program_id(1)
    ng = n_pages // PG
    rows = DSA_HEADS * TP
    scale = HD ** -0.5

    def idx_scores(kk):
        r = jnp.maximum(_nt(qidx_ref[...], kk.astype(BF16)), 0.0) * wb_ref[...]
        sc = r[0:TP]
        for h in range(1, IDX_HEADS):
            sc = sc + r[TP * h:TP * (h + 1)]
        return sc

    @pl.when(g == 0)
    def _():
        m_s[...] = jnp.full(m_s.shape, NEG, F32)
        l_s[...] = jnp.zeros(l_s.shape, F32)
        acc_s[...] = jnp.zeros(acc_s.shape, F32)

    @pl.when(g < ng)
    def _():
        for j in range(PG):
            key_s[g * PG + j] = _order_key(idx_scores(kpages[j][...]))

    @pl.when(g == ng - 1)
    def _():
        sc = jnp.where(_new_key_mask(TP, n_new), idx_scores(nkidx_ref[...]), -jnp.inf)
        key_s[n_pages] = _order_key(sc)

        def count_ge(cand):
            c = jnp.sum((key_s[...] >= cand[None]).astype(jnp.int32), axis=0)
            return jnp.sum(c, axis=-1, keepdims=True)

        thr_s[...] = jnp.broadcast_to(_kth_threshold(count_ge, TP, ksel), thr_s.shape)

    def attend(state, kv, bias, keep):
        sc = _nt(qc_ref[...], kv[:, 0:HD].astype(BF16)) * scale + bias
        return _softmax_step(*state, sc, keep, kv[:, HD:2 * HD].astype(BF16))

    def kept(pidx):
        return jnp.concatenate([key_s[pidx] >= thr_s[...]] * DSA_HEADS, axis=0)

    @pl.when(g >= ng)
    def _():
        state = (m_s[...], l_s[...], acc_s[...])
        for j in range(PG):
            pidx = (g - ng) * PG + j
            state = attend(state, vpages[j][...], tb_ref[jnp.where(pidx == n_pages - 1, 0, 2)], kept(pidx))
        m_s[...], l_s[...], acc_s[...] = state

    @pl.when(g == 2 * ng - 1)
    def _():
        state = (m_s[...], l_s[...], acc_s[...])
        _, l, acc = attend(state, nkv_ref[...], tb_ref[1], kept(n_pages) & _new_key_mask(rows, n_new))
        o = _finish(l, acc)
        for h in range(DSA_HEADS):
            o_ref[:, HD * h:HD * (h + 1)] = o[TP * h:TP * (h + 1)]


def _paged_call(body, page_table, ins, in_specs, out_shapes, out_specs, scratch, n_steps):
    nb = page_table.shape[0]
    return pl.pallas_call(
        body,
        grid_spec=pltpu.PrefetchScalarGridSpec(
            num_scalar_prefetch=1, grid=(nb, n_steps), in_specs=in_specs, out_specs=out_specs,
            scratch_shapes=scratch),
        out_shape=out_shapes,
        compiler_params=_cparams(("parallel", "arbitrary")),
    )(page_table, *ins)


def _softmax_scratch(rows, width):
    return [pltpu.VMEM((rows, 1), F32), pltpu.VMEM((rows, 1), F32), pltpu.VMEM((rows, width), F32)]


def _sample_attention(layer, page_table, n_new, q_a, q_c, q_idx, q_mla, small, w_idx, new, pools, state_win, tabs, wts):
    nb, n_pages = page_table.shape
    ng = n_pages // PG
    past = n_pages * PAGE
    rows4 = NSA_HEADS * TP
    nj = tabs["cb_s"].shape[1]
    nsp = tabs["cover_s"].shape[1]
    kv_page = (PAGE, 2 * HD)

    ins = [q_a] + [pools["cmp"]] * PG + [new["cmp"], wts["w3k"], wts["w3v"], wts["wn"], wts["g_cmp"],
                                        tabs["cb_s"], tabs["cover_s"]]
    specs = ([_seq_spec((rows4, HD))] + _page_specs(layer, kv_page, ng)
             + [_seq_spec((TP, 2 * HD))] + [_const_spec(a) for a in ins[PG + 2:]])
    o_cmp, sel = _paged_call(
        functools.partial(_s_cmp_kernel, n_pages=n_pages, n_new=n_new, n_pick=min(N_SEL, 2 * n_pages + 1)),
        page_table, ins, specs,
        [jax.ShapeDtypeStruct((nb, rows4, HD), F32), jax.ShapeDtypeStruct((nb, TP, nsp), F32)],
        [_seq_spec((rows4, HD)), _seq_spec((TP, nsp))],
        [pltpu.VMEM((nj, HD), F32), pltpu.VMEM((nj, HD), F32), pltpu.VMEM((8, HD), F32)], ng)

    wbuf = state_win.shape[2]
    ins = ([q_a, sel, o_cmp, small] + [pools["slc"]] * PG
           + [new["slc"], state_win, new["win"], tabs["tbs_a"], tabs["wb"]])
    specs = ([_seq_spec((rows4, HD)), _seq_spec((TP, nsp)), _seq_spec((rows4, HD)), _seq_spec((TP, LANE))]
             + _page_specs(layer, kv_page, ng)
             + [_seq_spec(kv_page),
                pl.BlockSpec((None, None, wbuf, 2 * HD), lambda b, g, pt: (layer, b, 0, 0)),
                _seq_spec(kv_page), _const_spec(tabs["tbs_a"]), _const_spec(tabs["wb"])])
    o_a = _paged_call(
        functools.partial(_s_slc_kernel, n_pages=n_pages, n_new=n_new),
        page_table, ins, specs, jax.ShapeDtypeStruct((nb, TP, NSA_HEADS * HD), F32),
        _seq_spec((TP, NSA_HEADS * HD)), _softmax_scratch(rows4, HD), ng)

    rows8 = MLA_HEADS * TP
    ins = ([q_mla] + [pools["mla"]] * PG + [pools["mla_inv"]] * PG + [new["mla"], new["mla_inv"], wts["wuv"]])
    specs = ([_seq_spec((rows8, ROWP))] + _page_specs(layer, (PAGE, MLA_ROW), ng)
             + _page_specs(layer, (MLA_HEADS, PAGE), ng)
             + [_seq_spec((PAGE, ROWP)), _seq_spec((MLA_HEADS, PAGE)), _const_spec(wts["wuv"])])
    o_b = _paged_call(
        functools.partial(_s_mla_kernel, n_new=n_new),
        page_table, ins, specs, jax.ShapeDtypeStruct((nb, TP, MLA_HEADS * HD), F32),
        _seq_spec((TP, MLA_HEADS * HD)), _softmax_scratch(rows8, KV_LORA), ng)

    rows16 = IDX_HEADS * TP
    ins = ([q_idx, w_idx, q_c] + [pools["kidx"]] * PG + [pools["dkv"]] * PG
           + [new["kidx"], new["dkv"], tabs["tbs_c"]])
    specs = ([_seq_spec((rows16, IDX_DIM)), _seq_spec((rows16, PAGE)), _seq_spec((rows4, HD))]
             + _page_specs(layer, (PAGE, IDX_DIM), ng) + _page_specs(layer, kv_page, ng, first_group=ng)
             + [_seq_spec((PAGE, IDX_DIM)), _seq_spec(kv_page), _const_spec(tabs["tbs_c"])])
    o_c = _paged_call(
        functools.partial(_s_dsa_kernel, n_pages=n_pages, n_new=n_new, ksel=min(IDX_TOPK, (past + n_new) // 4)),
        page_table, ins, specs, jax.ShapeDtypeStruct((nb, TP, DSA_HEADS * HD), F32),
        _seq_spec((TP, DSA_HEADS * HD)),
        [pltpu.VMEM((n_pages + 1, TP, PAGE), jnp.int32), pltpu.VMEM((TP, PAGE), jnp.int32)]
        + _softmax_scratch(rows4, HD), 2 * ng)
    return o_a, o_b, o_c


def _t5_bucket(n):
    max_exact = NUM_BUCKETS // 2
    nf = jnp.maximum(n, 1).astype(F32)
    large = max_exact + (jnp.log(nf / max_exact) / math.log(MAX_DISTANCE / max_exact)
                         * (NUM_BUCKETS - max_exact)).astype(jnp.int32)
    return jnp.where(n < max_exact, n, jnp.minimum(large, NUM_BUCKETS - 1))


def _round_up(x, m):
    return -(-x // m) * m


def _cover(n_rows, n_cols, first_row, n_valid, n_blocks):
    n = np.arange(n_rows)[:, None] - first_row
    s = np.arange(n_cols)[None]
    shared = np.maximum(np.minimum(CMP_STRIDE * n + CMP_LEN, SEL_BLOCK * s + SEL_BLOCK)
                        - np.maximum(CMP_STRIDE * n, SEL_BLOCK * s), 0)
    ok = (n >= 0) & (n < n_valid) & (s < n_blocks)
    return jnp.asarray(np.where(ok, shared / CMP_LEN, 0.0), F32)


def _tables(rel_bias, s, n_pages, n_new, wbuf):
    nq = s // QT
    past = n_pages * PAGE
    bd = rel_bias[_t5_bucket(jnp.arange(BIAS_CLIP + 1))].T.astype(F32)

    def look(heads, dist):
        return jnp.take(bd[heads], jnp.clip(dist, 0, BIAS_CLIP), axis=1)

    nsa, dsa = slice(0, NSA_HEADS), slice(NSA_HEADS, NSA_HEADS + DSA_HEADS)
    t = jnp.arange(QT)[:, None]
    c = jnp.arange(QT)[None]
    tb = lambda heads: jnp.stack([look(heads, QT * d + t - c) for d in range(3)]).reshape(3, -1, QT)
    ncp = s // CMP_STRIDE
    cb_p = look(nsa, jnp.arange(s)[:, None] - CMP_STRIDE * jnp.arange(ncp)[None] - (CMP_LEN - 1))
    cb_p = cb_p.reshape(NSA_HEADS, nq, QT, ncp).transpose(1, 0, 2, 3).reshape(nq, NSA_HEADS * QT, ncp)

    t8 = jnp.arange(TP)[:, None]
    n = jnp.arange(PAGE)[None]
    tbs = lambda heads: jnp.stack([look(heads, PAGE + t8 - n), look(heads, t8 - n),
                                   look(heads, jnp.full((TP, PAGE), BIAS_CLIP))]).reshape(3, -1, PAGE)
    nj = _round_up(8 * n_pages + 8, LANE)
    n_sel_blocks = -(-(past + n_new) // SEL_BLOCK)
    etab = (np.arange(s // SEL_BLOCK)[None, :, None]
            == (np.arange(nq)[:, None, None] * QT + np.arange(QT)[None, None, :]) // SEL_BLOCK)
    return dict(
        tb_a=tb(nsa), tb_c=tb(dsa), cb_p=cb_p,
        cover_p=_cover(ncp, s // SEL_BLOCK, 0, ncp - 1, s // SEL_BLOCK),
        etab=jnp.asarray(etab, BF16),
        tbs_a=tbs(nsa), tbs_c=tbs(dsa),
        wb=look(nsa, wbuf + t8 - jnp.arange(wbuf)[None]).reshape(-1, wbuf),
        cb_s=look(nsa, past + t8 - CMP_STRIDE * (jnp.arange(nj)[None] - 1) - (CMP_LEN - 1)).reshape(-1, nj),
        cover_s=_cover(nj, _round_up(n_sel_blocks, LANE), 1, 8 * n_pages, n_sel_blocks),
    )


_PROJ_SRC = np.concatenate([
    np.arange(0, 512), np.arange(512, 1280),
    np.arange(2124, 2636), np.arange(2636, 2892),
    np.arange(2892, 3916),
    np.arange(1292, 1804), np.arange(1804, 2060),
    np.arange(2060, 2124), np.arange(3916, 3980),
    np.arange(1280, 1292), np.arange(3980, 3996), np.full(LANE - GATE_W - IDX_HEADS, -1),
])


def _layer_weights(l, w_in, w_nsa_cmp, g_nsa_qk, g_mla_qlat, g_mla_kvlat, w_mla_uq, w_mla_uk, w_mla_uv,
                   g_mla_q, g_mla_k, g_dsa_qk, s):
    src = jnp.asarray(np.maximum(_PROJ_SRC, 0))
    w_in_p = jnp.where(jnp.asarray(_PROJ_SRC >= 0)[None], jnp.take(w_in[l], src, axis=1), 0.0).astype(BF16)

    def row(v):
        return jnp.pad(v.astype(F32), (0, Q_LORA - v.shape[0]))
    zeros = jnp.zeros((LANE - MLA_ROPE,), F32)
    gpack = jnp.stack([
        row(g_nsa_qk[l, 0]), row(g_nsa_qk[l, 2]), row(g_nsa_qk[l, 3]), row(g_dsa_qk[l, 0]), row(g_dsa_qk[l, 1]),
        row(g_mla_qlat[l]), row(g_mla_kvlat[l]), row(jnp.concatenate([g_mla_q[l], zeros])),
        row(g_mla_k[l, :MLA_NOPE]), row(jnp.concatenate([g_mla_k[l, MLA_NOPE:], zeros]))]
        + [jnp.zeros((Q_LORA,), F32)] * 6)

    wc = w_nsa_cmp[l].astype(F32)
    ncp = s // CMP_STRIDE
    off = np.arange(s)[None] - CMP_STRIDE * np.arange(ncp)[:, None]
    ok = (off >= 0) & (off < CMP_LEN) & (np.arange(ncp)[:, None] < ncp - 1)
    band = lambda k: jnp.where(jnp.asarray(ok), wc[k][jnp.asarray(np.clip(off, 0, CMP_LEN - 1))], 0.0).astype(BF16)
    r = np.arange(PAGE)[None]
    cc = np.arange(16)[:, None]
    tail_off = r - CMP_STRIDE * cc
    head_off = r - CMP_STRIDE * (cc - 1)
    tail_ok = (cc < 8) & (tail_off >= 0) & (tail_off < CMP_STRIDE)
    head_ok = (cc >= 1) & (cc <= 8) & (head_off >= 0) & (head_off < CMP_STRIDE)
    page_w = lambda k: (jnp.where(jnp.asarray(tail_ok), wc[k][jnp.asarray(np.clip(tail_off, 0, 15) + CMP_STRIDE)], 0.0)
                        + jnp.where(jnp.asarray(head_ok), wc[k][jnp.asarray(np.clip(head_off, 0, 15))], 0.0)).astype(BF16)
    wn = jnp.broadcast_to(wc[:, CMP_STRIDE:CMP_STRIDE + TP, None], (2, TP, HD))

    return dict(
        w_in=w_in_p, gpack=gpack,
        wuq2=jnp.pad(w_mla_uq[l], ((0, 0), (0, 0), (0, LANE - MLA_ROPE))).reshape(Q_LORA, -1).astype(BF16),
        wukt=w_mla_uk[l].transpose(1, 2, 0).astype(BF16),
        wukf=w_mla_uk[l].reshape(KV_LORA, -1).astype(BF16),
        wuv=w_mla_uv[l].transpose(1, 0, 2).astype(BF16),
        band_k=band(0), band_v=band(1), w3k=page_w(0), w3v=page_w(1), wn=wn,
        g_cmp=g_nsa_qk[l, 1].reshape(1, HD).astype(F32),
    )


def kernel(x_prompt, x_sample, cache_nsa_cmp_kv, cache_nsa_slc_kv, state_nsa_win_kv, cache_mla, cache_dsa_kv, cache_dsa_kidx, page_table, rel_bias, g_mix, w_in, g_nsa_qk, w_nsa_cmp, g_mla_qlat, g_mla_kvlat, w_mla_uq, w_mla_uk, w_mla_uv, g_mla_q, g_mla_k, g_dsa_qk, w_out, g_ffn, w_up, w_down):
    nb, s, _ = x_prompt.shape
    nd, n_new, _ = x_sample.shape
    depth, n_pool = cache_mla.shape[:2]
    n_pages = page_table.shape[1]
    past = n_pages * PAGE
    wbuf = state_nsa_win_kv.shape[2]
    assert s % QT == 0 and n_pages % PG == 0 and n_new <= TP and wbuf >= WINDOW
    mp, ms = nb * s, nd * n_new
    m_pad = _round_up(mp + ms, MM_T)
    pad_rows = lambda a: jnp.pad(a, ((0, m_pad - a.shape[0]), (0, 0)))
    smp = slice(mp, mp + ms)

    x = pad_rows(jnp.concatenate([x_prompt.reshape(mp, D_MODEL), x_sample.reshape(ms, D_MODEL)]))
    pos = jnp.concatenate([jnp.tile(jnp.arange(s), nb), jnp.tile(past + jnp.arange(n_new), nd)]).astype(F32)
    half = MLA_ROPE // 2
    ang = pos[:, None] * (ROPE_THETA ** (-jnp.arange(half, dtype=F32) / half))[None]
    cos, sin, z = jnp.cos(ang), jnp.sin(ang), jnp.zeros((mp + ms, LANE - MLA_ROPE), F32)
    cs = pad_rows(jnp.concatenate([cos, cos, z, -sin, sin, z], axis=1))

    tabs = _tables(rel_bias, s, n_pages, n_new, wbuf)
    pools = dict(
        cmp=cache_nsa_cmp_kv.reshape(depth, n_pool, PAGE, 2 * HD),
        slc=cache_nsa_slc_kv.reshape(depth, n_pool, PAGE, 2 * HD),
        dkv=cache_dsa_kv.reshape(depth, n_pool, PAGE, 2 * HD),
        kidx=cache_dsa_kidx, mla=cache_mla,
        mla_inv=jnp.swapaxes(cache_mla[..., KV_LORA + MLA_ROPE:], -1, -2))
    state_win = state_nsa_win_kv.reshape(depth, nd, wbuf, 2 * HD)

    def per_seq(a, rows):
        a = a.reshape(nd, n_new, a.shape[-1])
        return jnp.pad(a, ((0, 0), (0, rows - n_new), (0, 0)))

    def heads_first(a, n_heads):
        a = per_seq(a, TP).reshape(nd, TP, n_heads, -1)
        return a.transpose(0, 2, 1, 3).reshape(nd, n_heads * TP, -1)

    prompt_new, sample_new = [], []
    for l in range(depth):
        wts = _layer_weights(l, w_in, w_nsa_cmp, g_nsa_qk, g_mla_qlat, g_mla_kvlat, w_mla_uq, w_mla_uk,
                             w_mla_uv, g_mla_q, g_mla_k, g_dsa_qk, s)
        proj = _mm_full(x, wts["w_in"], gain=g_mix[l])
        qa, slc, win, kvc, qc, qidx, small, qmla, rowp = _post(
            proj, cs, wts["gpack"], wts["wuq2"], wts["wukt"], wts["wukf"])
        kvcmp = proj[:, C_CMP:C_CMP + 2 * HD]
        kidx = proj[:, C_A + MLA_ROPE:C_A + LANE]
        inv_r = rowp[:, KV_LORA + MLA_ROPE:MLA_ROW]

        ck, cv = _compress_prompt(kvcmp, wts["band_k"], wts["band_v"], wts["g_cmp"], nb, s)
        o_a = _nsa_prompt(qa, small, ck, cv, slc, win, tabs["tb_a"], tabs["cb_p"], tabs["cover_p"],
                          tabs["etab"], nb, s)
        inv_p = jnp.swapaxes(inv_r[:mp].reshape(nb, s // QT, QT, MLA_HEADS), -1, -2)
        o_b = _mla_prompt(qmla, rowp, inv_p, wts["wuv"], nb, s)
        kidx2 = jnp.concatenate([kidx[:mp], kidx[:mp]], axis=1).reshape(nb, s, LANE).astype(BF16)
        o_c = _dsa_prompt(qc, qidx, small, kidx2, kvc, tabs["tb_c"], nb, s)

        small_s = per_seq(small[smp], TP)
        w_idx = small_s[:, :, GATE_W:GATE_W + IDX_HEADS].transpose(0, 2, 1).reshape(nd, IDX_HEADS * TP, 1)
        new = dict(cmp=per_seq(kvcmp[smp], TP), slc=per_seq(slc[smp], PAGE), win=per_seq(win[smp], PAGE),
                   mla=per_seq(rowp[smp], PAGE), mla_inv=jnp.swapaxes(per_seq(inv_r[smp], PAGE), -1, -2),
                   kidx=per_seq(kidx[smp], PAGE), dkv=per_seq(kvc[smp], PAGE))
        so_a, so_b, so_c = _sample_attention(
            l, page_table, n_new, heads_first(qa[smp], NSA_HEADS), heads_first(qc[smp], DSA_HEADS),
            heads_first(qidx[smp], IDX_HEADS), heads_first(qmla[smp], MLA_HEADS), small_s,
            jnp.broadcast_to(w_idx, (nd, IDX_HEADS * TP, PAGE)), new, pools, state_win, tabs, wts)

        mixed = pad_rows(jnp.concatenate([
            jnp.concatenate([o_a, o_b, o_c], axis=1),
            jnp.concatenate([so_a[:, :n_new], so_b[:, :n_new], so_c[:, :n_new]], axis=2).reshape(ms, -1)]))
        x1 = _mm_full(mixed, w_out[l].astype(BF16), res=x)
        hidden = _mm_full(x1, w_up[l].astype(BF16), gain=g_ffn[l], act="relu2", out_dtype=BF16)
        x = _mm_ktiled(hidden, w_down[l].astype(BF16), x1)

        kv4 = lambda a, r, n: a[r].reshape(n, -1, 2, HD)
        pr = slice(0, mp)
        prompt_new.append((kv4(kvcmp, pr, nb), kv4(slc, pr, nb), kv4(win, pr, nb)[:, -min(WINDOW, s):],
                           rowp[pr, :MLA_ROW].reshape(nb, s, MLA_ROW), kv4(kvc, pr, nb),
                           kidx[pr].reshape(nb, s, IDX_DIM)))
        win_all = jnp.concatenate([state_nsa_win_kv[l], kv4(win, smp, nd)], axis=1)
        sample_new.append((kv4(kvcmp, smp, nd), kv4(slc, smp, nd), win_all[:, -min(WINDOW, past + n_new):],
                           rowp[smp, :MLA_ROW].reshape(nd, n_new, MLA_ROW), kv4(kvc, smp, nd),
                           kidx[smp].reshape(nd, n_new, IDX_DIM)))

    outs_p = [jnp.stack(a) for a in zip(*prompt_new)]
    outs_s = [jnp.stack(a) for a in zip(*sample_new)]
    return (x[:mp].reshape(nb, s, D_MODEL), x[smp].reshape(nd, n_new, D_MODEL), *outs_p, *outs_s)
```

```python
import functools
import math

import numpy as np
import jax
import jax.numpy as jnp
from jax import lax
from jax.experimental import pallas as pl
from jax.experimental.pallas import tpu as pltpu

F32 = jnp.float32
BF16 = jnp.bfloat16

D_MODEL = 2048
PAGE = 128
NSA_HEADS = 4
HD = 128
CMP_LEN = 32
CMP_STRIDE = 16
SEL_BLOCK = 64
N_SEL = 16
WINDOW = 512
MLA_HEADS = 8
MLA_NOPE = 128
MLA_ROPE = 64
MLA_QK = MLA_NOPE + MLA_ROPE
Q_LORA = 512
KV_LORA = 256
MLA_ROW = KV_LORA + MLA_ROPE + MLA_HEADS
DSA_HEADS = 4
IDX_HEADS = 16
IDX_DIM = 64
IDX_TOPK = 256
NUM_BUCKETS = 32
MAX_DISTANCE = 128
ROPE_THETA = 10000.0
EPS = 1e-6
NEG = -1e30
FORCE_BONUS = 1e4

LANE = 128
QT = 128
KC = 4
TP = 8
PG = 8
ROWP = 384
MM_T = 512
MM_TK = 2048
POST_T = 256
VMEM_LIMIT = 56 * 1024 * 1024
BIAS_CLIP = 255

C_QA, C_CMP, C_SLC, C_WIN, C_QC, C_KVC, C_QIDX, C_DQ, C_DKV, C_A, C_B, PROJ_P = (
    0, 512, 768, 1024, 1280, 1792, 2048, 3072, 3584, 3840, 3968, 4096)
GATE_W = 3 * NSA_HEADS


def _nt(a, b):
    return lax.dot_general(a, b, (((1,), (1,)), ((), ())), preferred_element_type=F32)


def _mm(a, b):
    return jnp.dot(a, b, preferred_element_type=F32)


def _cparams(sem):
    return pltpu.CompilerParams(dimension_semantics=sem, vmem_limit_bytes=VMEM_LIMIT)


def _softmax_step(m, l, acc, s, mask, v):
    if mask is not None:
        s = jnp.where(mask, s, NEG)
    m_new = jnp.maximum(m, jnp.max(s, axis=-1, keepdims=True))
    alpha = jnp.exp(m - m_new)
    e = jnp.exp(s - m_new)
    if mask is not None:
        e = jnp.where(mask, e, 0.0)
    l = alpha * l + jnp.sum(e, axis=-1, keepdims=True)
    e = e.astype(BF16)
    acc = alpha * acc + (v(e) if callable(v) else _mm(e, v))
    return m_new, l, acc


def _by_block(blocks, fn):
    def pv(e):
        out = fn(e[:, 0:QT], blocks[0])
        for j in range(1, len(blocks)):
            out = out + fn(e[:, QT * j:QT * (j + 1)], blocks[j])
        return out
    return pv


def _finish(l, acc):
    return acc / jnp.maximum(l, 1e-30)


def _top_blocks(score, n_pick):
    lane = lax.broadcasted_iota(jnp.int32, score.shape, 1)
    width = score.shape[1]
    sel = jnp.zeros(score.shape, F32)
    for _ in range(n_pick):
        mx = jnp.max(score, axis=-1, keepdims=True)
        idx = jnp.min(jnp.where(score == mx, lane, width), axis=-1, keepdims=True)
        hit = lane == idx
        sel = jnp.where(hit, 1.0, sel)
        score = jnp.where(hit, -jnp.inf, score)
    return sel


def _order_key(x):
    bits = lax.bitcast_convert_type(x, jnp.int32)
    return jnp.where(bits < 0, bits ^ jnp.int32(0x7FFFFFFF), bits)


def _kth_threshold(count_ge, rows, k):
    def step(i, t):
        cand = t + lax.shift_left(jnp.int32(1), jnp.int32(31) - i)
        return jnp.where(count_ge(cand) >= k, cand, t)
    t0 = jnp.full((rows, 1), jnp.iinfo(jnp.int32).min, jnp.int32)
    return lax.fori_loop(0, 32, step, t0)


def _mm_full_kernel(*refs, has_gain, has_res, act):
    a_ref, w_ref = refs[0], refs[1]
    k = 2
    g_ref = r_ref = None
    if has_gain:
        g_ref = refs[k]; k += 1
    if has_res:
        r_ref = refs[k]; k += 1
    o_ref, abuf = refs[k], refs[k + 1]

    @pl.when(pl.program_id(1) == 0)
    def _():
        a = a_ref[...].astype(F32)
        if has_gain:
            ms = jnp.mean(a * a, axis=-1, keepdims=True)
            a = a * lax.rsqrt(ms + EPS) * g_ref[...]
        abuf[...] = a.astype(BF16)

    y = _mm(abuf[...], w_ref[...])
    if act == "relu2":
        y = jnp.square(jnp.maximum(y, 0.0))
    if has_res:
        y = y + r_ref[...]
    o_ref[...] = y.astype(o_ref.dtype)


def _mm_full(a, w, gain=None, res=None, act=None, out_dtype=F32):
    m, kdim = a.shape
    n = w.shape[1]
    ins = [a, w]
    specs = [pl.BlockSpec((MM_T, kdim), lambda i, j: (i, 0)),
             pl.BlockSpec((kdim, MM_T), lambda i, j: (0, j))]
    if gain is not None:
        ins.append(gain.reshape(1, kdim).astype(F32))
        specs.append(pl.BlockSpec((1, kdim), lambda i, j: (0, 0)))
    if res is not None:
        ins.append(res)
        specs.append(pl.BlockSpec((MM_T, MM_T), lambda i, j: (i, j)))
    return pl.pallas_call(
        functools.partial(_mm_full_kernel, has_gain=gain is not None, has_res=res is not None, act=act),
        grid=(m // MM_T, n // MM_T),
        in_specs=specs,
        out_specs=pl.BlockSpec((MM_T, MM_T), lambda i, j: (i, j)),
        out_shape=jax.ShapeDtypeStruct((m, n), out_dtype),
        scratch_shapes=[pltpu.VMEM((MM_T, kdim), BF16)],
        compiler_params=_cparams(("parallel", "arbitrary")),
        name="matmul_" + (act or ("norm" if gain is not None else "plain")),
    )(*ins)


def _mm_kt_kernel(a_ref, w_ref, r_ref, o_ref, acc):
    kk = pl.program_id(2)

    @pl.when(kk == 0)
    def _():
        acc[...] = r_ref[...]

    acc[...] += _mm(a_ref[...], w_ref[...])

    @pl.when(kk == pl.num_programs(2) - 1)
    def _():
        o_ref[...] = acc[...]


def _mm_ktiled(a, w, res):
    m, kdim = a.shape
    n = w.shape[1]
    return pl.pallas_call(
        _mm_kt_kernel,
        grid=(m // MM_T, n // MM_T, kdim // MM_TK),
        in_specs=[pl.BlockSpec((MM_T, MM_TK), lambda i, j, k: (i, k)),
                  pl.BlockSpec((MM_TK, MM_T), lambda i, j, k: (k, j)),
                  pl.BlockSpec((MM_T, MM_T), lambda i, j, k: (i, j))],
        out_specs=pl.BlockSpec((MM_T, MM_T), lambda i, j, k: (i, j)),
        out_shape=jax.ShapeDtypeStruct((m, n), F32),
        scratch_shapes=[pltpu.VMEM((MM_T, MM_T), F32)],
        compiler_params=_cparams(("parallel", "parallel", "arbitrary")),
        name="matmul_ktiled",
    )(a, w, res)


def _post_kernel(p_ref, cs_ref, g_ref, wuq_ref, wukt_ref, wukf_ref,
                 qa_ref, slc_ref, win_ref, kvc_ref, qc_ref, qidx_ref, small_ref, qmla_ref, row_ref):
    def nrm(x, g):
        ms = jnp.mean(x * x, axis=-1, keepdims=True)
        return x * lax.rsqrt(ms + EPS) * g

    def gain(r, w):
        return g_ref[r:r + 1, 0:w]

    for h in range(NSA_HEADS):
        qa_ref[:, HD * h:HD * (h + 1)] = nrm(p_ref[:, C_QA + HD * h:C_QA + HD * (h + 1)], gain(0, HD)).astype(BF16)
        qc_ref[:, HD * h:HD * (h + 1)] = nrm(p_ref[:, C_QC + HD * h:C_QC + HD * (h + 1)], gain(3, HD)).astype(BF16)
    for ref, col, r in ((slc_ref, C_SLC, 1), (win_ref, C_WIN, 2), (kvc_ref, C_KVC, 4)):
        ref[:, 0:HD] = nrm(p_ref[:, col:col + HD], gain(r, HD))
        ref[:, HD:2 * HD] = p_ref[:, col + HD:col + 2 * HD]
    qidx_ref[...] = (p_ref[:, C_QIDX:C_QIDX + IDX_HEADS * IDX_DIM] * IDX_DIM ** -0.5).astype(BF16)

    tm = p_ref.shape[0]
    lane = lax.broadcasted_iota(jnp.int32, (tm, LANE), 1)
    blk_b = p_ref[:, C_B:C_B + LANE]
    small_ref[...] = jnp.where(lane < GATE_W, jax.nn.sigmoid(blk_b), blk_b * IDX_HEADS ** -0.5)

    cos = cs_ref[:, 0:LANE]
    sin = cs_ref[:, LANE:2 * LANE]

    def rope(x):
        swapped = jnp.where(lane < MLA_ROPE // 2, pltpu.roll(x, LANE - MLA_ROPE // 2, 1),
                            pltpu.roll(x, MLA_ROPE // 2, 1))
        return x * cos + swapped * sin

    cq = nrm(p_ref[:, C_DQ:C_DQ + Q_LORA], gain(5, Q_LORA)).astype(BF16)
    q2 = _mm(cq, wuq_ref[...])
    for h in range(MLA_HEADS):
        blk = q2[:, 2 * LANE * h:2 * LANE * (h + 1)]
        ms = jnp.sum(blk * blk, axis=-1, keepdims=True) * (1.0 / MLA_QK)
        qn = blk * lax.rsqrt(ms + EPS) * gain(7, 2 * LANE)
        qlat = _mm((qn[:, 0:MLA_NOPE] * gain(8, MLA_NOPE)).astype(BF16), wukt_ref[h])
        qmla_ref[:, ROWP * h:ROWP * h + KV_LORA] = qlat.astype(BF16)
        qmla_ref[:, ROWP * h + KV_LORA:ROWP * (h + 1)] = rope(qn[:, MLA_NOPE:2 * LANE]).astype(BF16)

    ckv = nrm(p_ref[:, C_DKV:C_DKV + KV_LORA], gain(6, KV_LORA))
    kn = _mm(ckv.astype(BF16), wukf_ref[...])
    kpe = jnp.where(lane < MLA_ROPE, p_ref[:, C_A:C_A + LANE], 0.0)
    ssq_pe = jnp.sum(kpe * kpe, axis=-1, keepdims=True)
    third = rope(kpe * gain(9, LANE))
    for h in range(MLA_HEADS):
        kh = kn[:, MLA_NOPE * h:MLA_NOPE * (h + 1)]
        ms = (jnp.sum(kh * kh, axis=-1, keepdims=True) + ssq_pe) * (1.0 / MLA_QK)
        third = jnp.where(lane == MLA_ROPE + h, lax.rsqrt(ms + EPS), third)
    row_ref[:, 0:KV_LORA] = ckv
    row_ref[:, KV_LORA:ROWP] = third


def _post(proj, cs, gpack, wuq2, wukt, wukf):
    m = proj.shape[0]
    row = lambda w: pl.BlockSpec((POST_T, w), lambda i: (i, 0))
    full = lambda a: pl.BlockSpec(a.shape, lambda i: (0,) * a.ndim)
    widths = (4 * HD, 2 * HD, 2 * HD, 2 * HD, 4 * HD, IDX_HEADS * IDX_DIM, LANE, MLA_HEADS * ROWP, ROWP)
    dtypes = (BF16, F32, F32, F32, BF16, BF16, F32, BF16, F32)
    return pl.pallas_call(
        _post_kernel,
        grid=(m // POST_T,),
        in_specs=[row(PROJ_P), row(2 * LANE), full(gpack), full(wuq2), full(wukt), full(wukf)],
        out_specs=[row(w) for w in widths],
        out_shape=[jax.ShapeDtypeStruct((m, w), d) for w, d in zip(widths, dtypes)],
        compiler_params=_cparams(("parallel",)),
        name="post_projection",
    )(proj, cs, gpack, wuq2, wukt, wukf)


def _compress_kernel(kv_ref, wk_ref, wv_ref, g_ref, ck_ref, cv_ref):
    ck = _mm(wk_ref[...], kv_ref[:, 0:HD].astype(BF16))
    ms = jnp.mean(ck * ck, axis=-1, keepdims=True)
    ck_ref[...] = (ck * lax.rsqrt(ms + EPS) * g_ref[...]).astype(BF16)
    cv_ref[...] = _mm(wv_ref[...], kv_ref[:, HD:2 * HD].astype(BF16)).astype(BF16)


def _compress_prompt(kvcmp, wk, wv, g, nb, s):
    ncp = wk.shape[0]
    return pl.pallas_call(
        _compress_kernel,
        grid=(nb,),
        in_specs=[pl.BlockSpec((s, 2 * HD), lambda b: (b, 0)),
                  pl.BlockSpec(wk.shape, lambda b: (0, 0)),
                  pl.BlockSpec(wv.shape, lambda b: (0, 0)),
                  pl.BlockSpec((1, HD), lambda b: (0, 0))],
        out_specs=[pl.BlockSpec((None, ncp, HD), lambda b: (b, 0, 0))] * 2,
        out_shape=[jax.ShapeDtypeStruct((nb, ncp, HD), BF16)] * 2,
        compiler_params=_cparams(("parallel",)),
        name="compress_prompt",
    )(kvcmp, wk, wv, g)


def _tile_dist(qi, kc, rows):
    t = lax.broadcasted_iota(jnp.int32, (rows, QT), 0) % QT
    c = lax.broadcasted_iota(jnp.int32, (rows, QT), 1)
    return QT * (qi - kc) + t - c


def _nsa_prompt_kernel(qa_ref, small_ref, ck_ref, cv_ref, slc_ref, win_ref, tb_ref, cb_ref, cover_ref,
                       e_ref, o_ref, *, n_cmp, n_pick):
    qi = pl.program_id(1)
    rows = NSA_HEADS * QT
    scale = HD ** -0.5
    q4 = jnp.concatenate([qa_ref[:, HD * h:HD * (h + 1)] for h in range(NSA_HEADS)], axis=0)

    ncp = ck_ref.shape[0]
    s = _nt(q4, ck_ref[...]) * scale + cb_ref[...]
    t = lax.broadcasted_iota(jnp.int32, (rows, ncp), 0) % QT
    ci = lax.broadcasted_iota(jnp.int32, (rows, ncp), 1)
    mask = (QT * qi + t - CMP_STRIDE * ci - (CMP_LEN - 1) >= 0) & (ci < n_cmp)
    s = jnp.where(mask, s, NEG)
    e = jnp.where(mask, jnp.exp(s - jnp.max(s, axis=-1, keepdims=True)), 0.0)
    p = e / jnp.maximum(jnp.sum(e, axis=-1, keepdims=True), 1e-30)
    o_cmp = _mm(p.astype(BF16), cv_ref[...])
    psum = p[0:QT] + p[QT:2 * QT] + p[2 * QT:3 * QT] + p[3 * QT:4 * QT]
    imp = jnp.dot(psum, cover_ref[...], preferred_element_type=F32, precision=lax.Precision.HIGHEST)
    ns = imp.shape[1]
    blk = lax.broadcasted_iota(jnp.int32, (QT, ns), 1)
    qb = (QT * qi + lax.broadcasted_iota(jnp.int32, (QT, ns), 0)) // SEL_BLOCK
    forced = (blk == 0) | (blk == qb) | (blk == qb - 1)
    score = jnp.where(blk <= qb, imp + jnp.where(forced, FORCE_BONUS, 0.0), -jnp.inf)
    sel = _top_blocks(score, n_pick).astype(BF16)

    init = (jnp.full((rows, 1), NEG, F32), jnp.zeros((rows, 1), F32), jnp.zeros((rows, HD), F32))

    def logits(ref, kcs):
        ks, vs = [], []
        for kc in kcs:
            kv = ref[pl.ds(pl.multiple_of(kc * QT, QT), QT), :]
            ks.append(kv[:, 0:HD].astype(BF16))
            vs.append(kv[:, HD:2 * HD].astype(BF16))
        sc = jnp.concatenate([_nt(q4, k) for k in ks], axis=1) * scale
        bias = jnp.concatenate([tb_ref[jnp.clip(qi - kc, 0, 2)] for kc in kcs], axis=1)
        return sc + bias, vs

    def slc_body(i, carry):
        kcs = [i * KC + u for u in range(KC)]
        sc, vs = logits(slc_ref, kcs)
        picked = jnp.concatenate([_mm(sel, e_ref[kc]) for kc in kcs], axis=1)
        picked = jnp.concatenate([picked] * NSA_HEADS, axis=0)
        dist = jnp.concatenate([_tile_dist(qi, kc, rows) for kc in kcs], axis=1)
        return _softmax_step(*carry, sc, (picked > 0.5) & (dist >= 0), _by_block(vs, _mm))

    _, l, acc = lax.fori_loop(0, qi // KC + 1, slc_body, init)
    o_slc = _finish(l, acc)

    first = qi - WINDOW // QT
    kcs = [first + u for u in range(WINDOW // QT + 1)]
    sc, vs = logits(win_ref, [jnp.maximum(kc, 0) for kc in kcs])
    ok = []
    for kc in kcs:
        dist = _tile_dist(qi, kc, rows)
        ok.append((dist >= 0) & (dist <= WINDOW) & (kc >= 0))
    _, l, acc = _softmax_step(*init, sc, jnp.concatenate(ok, axis=1), _by_block(vs, _mm))
    o_win = _finish(l, acc)

    gate = small_ref[...]
    for h in range(NSA_HEADS):
        r = slice(QT * h, QT * (h + 1))
        o_ref[:, HD * h:HD * (h + 1)] = (gate[:, 3 * h:3 * h + 1] * o_cmp[r] + gate[:, 3 * h + 1:3 * h + 2] * o_slc[r]
                                         + gate[:, 3 * h + 2:3 * h + 3] * o_win[r])


def _nsa_prompt(qa, small, ck, cv, slc, win, tb, cb, cover, etab, nb, s):
    nq = s // QT
    ncp = ck.shape[1]
    rows = NSA_HEADS * QT
    tile = lambda w: pl.BlockSpec((QT, w), lambda b, i: (b * nq + i, 0))
    seq = lambda w: pl.BlockSpec((s, w), lambda b, i: (b, 0))
    const = lambda a: pl.BlockSpec(a.shape, lambda b, i: (0,) * a.ndim)
    return pl.pallas_call(
        functools.partial(_nsa_prompt_kernel, n_cmp=s // CMP_STRIDE - 1, n_pick=min(N_SEL, s // SEL_BLOCK)),
        grid=(nb, nq),
        in_specs=[tile(4 * HD), tile(LANE),
                  pl.BlockSpec((None, ncp, HD), lambda b, i: (b, 0, 0)),
                  pl.BlockSpec((None, ncp, HD), lambda b, i: (b, 0, 0)),
                  seq(2 * HD), seq(2 * HD), const(tb),
                  pl.BlockSpec((None, rows, ncp), lambda b, i: (i, 0, 0)),
                  const(cover), const(etab)],
        out_specs=pl.BlockSpec((QT, 4 * HD), lambda b, i: (b * nq + i, 0)),
        out_shape=jax.ShapeDtypeStruct((nb * s, 4 * HD), F32),
        compiler_params=_cparams(("parallel", "arbitrary")),
        name="nsa_prompt",
    )(qa, small, ck, cv, slc, win, tb, cb, cover, etab)


def _mla_prompt_kernel(q_ref, row_ref, inv_ref, wuv_ref, o_ref, m_s, l_s, acc_s):
    qi = pl.program_id(1)
    rows = MLA_HEADS * QT
    scale = MLA_QK ** -0.5
    q8 = jnp.concatenate([q_ref[:, ROWP * h:ROWP * (h + 1)] for h in range(MLA_HEADS)], axis=0)
    m_s[...] = jnp.full((rows, 1), NEG, F32)
    l_s[...] = jnp.zeros((rows, 1), F32)
    acc_s[...] = jnp.zeros((rows, KV_LORA), F32)

    def step(i, causal):
        kcs = [i * KC + u for u in range(KC)]
        tiles = [row_ref[pl.ds(pl.multiple_of(kc * QT, QT), QT), :].astype(BF16) for kc in kcs]
        inv = jnp.concatenate([inv_ref[kc] for kc in kcs], axis=1)
        mult = jnp.concatenate([jnp.broadcast_to(inv[h:h + 1], (QT, KC * QT)) for h in range(MLA_HEADS)], axis=0)
        sc = jnp.concatenate([_nt(q8, t) for t in tiles], axis=1) * (mult * scale)
        mask = jnp.concatenate([_tile_dist(qi, kc, rows) for kc in kcs], axis=1) >= 0 if causal else None
        m_, l_, a_ = _softmax_step(m_s[...], l_s[...], acc_s[...], sc, mask,
                                   _by_block([t[:, 0:KV_LORA] for t in tiles], _mm))
        m_s[...] = m_
        l_s[...] = l_
        acc_s[...] = a_

    def body(i, _):
        step(i, False)
        return 0

    lax.fori_loop(0, qi // KC, body, 0)
    step(qi // KC, True)
    o_lat = _finish(l_s[...], acc_s[...]).astype(BF16)
    for h in range(MLA_HEADS):
        o_ref[:, HD * h:HD * (h + 1)] = _mm(o_lat[QT * h:QT * (h + 1)], wuv_ref[h])


def _mla_prompt(qmla, rowp, invt, wuv, nb, s):
    nq = s // QT
    rows = MLA_HEADS * QT
    return pl.pallas_call(
        _mla_prompt_kernel,
        grid=(nb, nq),
        in_specs=[pl.BlockSpec((QT, MLA_HEADS * ROWP), lambda b, i: (b * nq + i, 0)),
                  pl.BlockSpec((s, ROWP), lambda b, i: (b, 0)),
                  pl.BlockSpec((None, nq, MLA_HEADS, QT), lambda b, i: (b, 0, 0, 0)),
                  pl.BlockSpec(wuv.shape, lambda b, i: (0, 0, 0))],
        out_specs=pl.BlockSpec((QT, MLA_HEADS * HD), lambda b, i: (b * nq + i, 0)),
        out_shape=jax.ShapeDtypeStruct((nb * s, MLA_HEADS * HD), F32),
        scratch_shapes=[pltpu.VMEM((rows, 1), F32), pltpu.VMEM((rows, 1), F32), pltpu.VMEM((rows, KV_LORA), F32)],
        compiler_params=_cparams(("parallel", "arbitrary")),
        name="mla_prompt",
    )(qmla, rowp, invt, wuv)


def _dsa_prompt_kernel(qc_ref, qidx_ref, small_ref, kidx_ref, kvc_ref, tb_ref, o_ref, key_s, *, ksel):
    qi = pl.program_id(1)
    rows = DSA_HEADS * QT
    scale = HD ** -0.5
    lane = lax.broadcasted_iota(jnp.int32, (QT, LANE), 1)
    w = small_ref[...]

    heads, wcols = [], []
    for j in range(IDX_HEADS // 2):
        x = qidx_ref[:, LANE * j:LANE * (j + 1)]
        heads += [jnp.where(lane < IDX_DIM, x, 0).astype(BF16), jnp.where(lane >= IDX_DIM, x, 0).astype(BF16)]
        wcols += [w[:, GATE_W + 2 * j:GATE_W + 2 * j + 1], w[:, GATE_W + 2 * j + 1:GATE_W + 2 * j + 2]]
    q16 = jnp.concatenate(heads, axis=0)
    wcol = jnp.concatenate(wcols, axis=0)
    n_groups = qi // KC + 1

    def idx_body(i, _):
        for u in range(KC):
            kc = i * KC + u
            kk = kidx_ref[pl.ds(pl.multiple_of(kc * QT, QT), QT), :]
            r = jnp.maximum(_nt(q16, kk), 0.0) * wcol
            acc = r[0:QT]
            for h in range(1, IDX_HEADS):
                acc = acc + r[QT * h:QT * (h + 1)]
            key_s[kc] = _order_key(jnp.where(_tile_dist(qi, kc, QT) >= 0, acc, -jnp.inf))
        return 0

    lax.fori_loop(0, n_groups, idx_body, 0)

    def count_ge(cand):
        def body(i, c):
            for u in range(KC):
                c = c + (key_s[i * KC + u] >= cand).astype(jnp.int32)
            return c
        c = lax.fori_loop(0, n_groups, body, jnp.zeros((QT, QT), jnp.int32))
        return jnp.sum(c, axis=-1, keepdims=True)

    thr = _kth_threshold(count_ge, QT, ksel)
    q4 = jnp.concatenate([qc_ref[:, HD * h:HD * (h + 1)] for h in range(DSA_HEADS)], axis=0)

    def att_body(i, carry):
        kcs = [i * KC + u for u in range(KC)]
        ks, vs = [], []
        for kc in kcs:
            kv = kvc_ref[pl.ds(pl.multiple_of(kc * QT, QT), QT), :]
            ks.append(kv[:, 0:HD].astype(BF16))
            vs.append(kv[:, HD:2 * HD].astype(BF16))
        sc = jnp.concatenate([_nt(q4, k) for k in ks], axis=1) * scale
        sc = sc + jnp.concatenate([tb_ref[jnp.clip(qi - kc, 0, 2)] for kc in kcs], axis=1)
        keep = jnp.concatenate([key_s[kc] >= thr for kc in kcs], axis=1)
        dist = jnp.concatenate([_tile_dist(qi, kc, rows) for kc in kcs], axis=1)
        keep = jnp.concatenate([keep] * DSA_HEADS, axis=0) & (dist >= 0)
        return _softmax_step(*carry, sc, keep, _by_block(vs, _mm))

    init = (jnp.full((rows, 1), NEG, F32), jnp.zeros((rows, 1), F32), jnp.zeros((rows, HD), F32))
    _, l, acc = lax.fori_loop(0, n_groups, att_body, init)
    o = _finish(l, acc)
    for h in range(DSA_HEADS):
        o_ref[:, HD * h:HD * (h + 1)] = o[QT * h:QT * (h + 1)]


def _dsa_prompt(qc, qidx, small, kidx2, kvc, tb, nb, s):
    nq = s // QT
    tile = lambda w: pl.BlockSpec((QT, w), lambda b, i: (b * nq + i, 0))
    return pl.pallas_call(
        functools.partial(_dsa_prompt_kernel, ksel=min(IDX_TOPK, s // 4)),
        grid=(nb, nq),
        in_specs=[tile(4 * HD), tile(IDX_HEADS * IDX_DIM), tile(LANE),
                  pl.BlockSpec((None, s, LANE), lambda b, i: (b, 0, 0)),
                  pl.BlockSpec((s, 2 * HD), lambda b, i: (b, 0)),
                  pl.BlockSpec(tb.shape, lambda b, i: (0, 0, 0))],
        out_specs=pl.BlockSpec((QT, 4 * HD), lambda b, i: (b * nq + i, 0)),
        out_shape=jax.ShapeDtypeStruct((nb * s, 4 * HD), F32),
        scratch_shapes=[pltpu.VMEM((nq, QT, QT), jnp.int32)],
        compiler_params=_cparams(("parallel", "arbitrary")),
        name="dsa_prompt",
    )(qc, qidx, small, kidx2, kvc, tb)


def _page_specs(layer, page_shape, n_groups, first_group=0):
    def index(b, g, pt, *, j):
        gg = jnp.clip(g - first_group, 0, n_groups - 1)
        return (layer, pt[b, gg * PG + j], 0, 0)
    return [pl.BlockSpec((None, None) + tuple(page_shape), functools.partial(index, j=j)) for j in range(PG)]


def _seq_spec(shape):
    return pl.BlockSpec((None,) + tuple(shape), lambda b, g, pt: (b,) + (0,) * len(shape))


def _const_spec(a):
    return pl.BlockSpec(a.shape, lambda b, g, pt: (0,) * a.ndim)


def _split_kv(ref):
    n = ref.shape[0] // 2
    return ref[pl.ds(0, n, stride=2), :].astype(BF16), ref[pl.ds(1, n, stride=2), :].astype(BF16)


def _new_key_mask(rows, n_new):
    t = lax.broadcasted_iota(jnp.int32, (rows, PAGE), 0) % TP
    n = lax.broadcasted_iota(jnp.int32, (rows, PAGE), 1)
    return (n <= t) & (n < n_new)


def _s_cmp_kernel(pt_ref, q_ref, *refs, n_pages, n_new, n_pick):
    pages = refs[:PG]
    (new_ref, w3_ref, wn_ref, g_ref, cb_ref, cover_ref,
     ocmp_ref, sel_ref, ck_s, cv_s, carry_s) = refs[PG:]
    g = pl.program_id(1)
    rows = NSA_HEADS * TP
    nj = ck_s.shape[0]
    past = n_pages * PAGE
    row8 = lax.broadcasted_iota(jnp.int32, (8, HD), 0)

    @pl.when(g == 0)
    def _():
        ck_s[...] = jnp.zeros(ck_s.shape, F32)
        cv_s[...] = jnp.zeros(cv_s.shape, F32)
        carry_s[...] = jnp.zeros(carry_s.shape, F32)

    carry_k = carry_s[0:1, :]
    carry_v = carry_s[1:2, :]
    for j in range(PG):
        r = _mm(w3_ref[...], pages[j][...].astype(BF16))
        rk, rv = r[0:16], r[16:32]
        off = pl.multiple_of(g * (PG * 8) + 8 * j, 8)
        ck_s[pl.ds(off, 8), :] = rk[0:8] + jnp.where(row8 == 0, carry_k, 0.0)
        cv_s[pl.ds(off, 8), :] = rv[0:8] + jnp.where(row8 == 0, carry_v, 0.0)
        carry_k = rk[8:9]
        carry_v = rv[8:9]
    carry_s[0:1, :] = carry_k
    carry_s[1:2, :] = carry_v

    @pl.when(g == pl.num_programs(1) - 1)
    def _():
        new = new_ref[...]
        tail_k = jnp.sum(new[:, 0:HD] * wn_ref[0], axis=0, keepdims=True) + carry_k
        tail_v = jnp.sum(new[:, HD:2 * HD] * wn_ref[1], axis=0, keepdims=True) + carry_v
        last = 8 * n_pages
        ck_s[last:last + 8, :] = jnp.where(row8 == 0, tail_k, 0.0)
        cv_s[last:last + 8, :] = jnp.where(row8 == 0, tail_v, 0.0)

        ck = ck_s[...]
        ms = jnp.mean(ck * ck, axis=-1, keepdims=True)
        ckn = (ck * lax.rsqrt(ms + EPS) * g_ref[...]).astype(BF16)
        s = _nt(q_ref[...], ckn) * HD ** -0.5 + cb_ref[...]
        t = lax.broadcasted_iota(jnp.int32, (rows, nj), 0) % TP
        jj = lax.broadcasted_iota(jnp.int32, (rows, nj), 1)
        mask = (jj >= 1) & (jj <= last) & (past + t - CMP_STRIDE * (jj - 1) - (CMP_LEN - 1) >= 0)
        s = jnp.where(mask, s, NEG)
        e = jnp.where(mask, jnp.exp(s - jnp.max(s, axis=-1, keepdims=True)), 0.0)
        p = e / jnp.maximum(jnp.sum(e, axis=-1, keepdims=True), 1e-30)
        ocmp_ref[...] = _mm(p.astype(BF16), cv_s[...].astype(BF16))
        psum = p[0:TP] + p[TP:2 * TP] + p[2 * TP:3 * TP] + p[3 * TP:4 * TP]
        imp = jnp.dot(psum, cover_ref[...], preferred_element_type=F32, precision=lax.Precision.HIGHEST)
        nsp = imp.shape[1]
        blk = lax.broadcasted_iota(jnp.int32, (TP, nsp), 1)
        qb = (past + lax.broadcasted_iota(jnp.int32, (TP, nsp), 0)) // SEL_BLOCK
        forced = (blk == 0) | (blk == qb) | (blk == qb - 1)
        score = jnp.where(blk <= qb, imp + jnp.where(forced, FORCE_BONUS, 0.0), -jnp.inf)
        sel_ref[...] = _top_blocks(score, n_pick)


def _s_slc_kernel(pt_ref, q_ref, sel_ref, ocmp_ref, small_ref, *refs, n_pages, n_new):
    pages = refs[:PG]
    (nslc_ref, win_ref, nwin_ref, tb_ref, wb_ref, o_ref, m_s, l_s, acc_s) = refs[PG:]
    g = pl.program_id(1)
    rows = NSA_HEADS * TP
    scale = HD ** -0.5
    q = q_ref[...]
    sel = sel_ref[...]
    nsp = sel.shape[1]
    lane_s = lax.broadcasted_iota(jnp.int32, (TP, nsp), 1)
    lane_k = lax.broadcasted_iota(jnp.int32, (TP, PAGE), 1)

    @pl.when(g == 0)
    def _():
        m_s[...] = jnp.full(m_s.shape, NEG, F32)
        l_s[...] = jnp.zeros(l_s.shape, F32)
        acc_s[...] = jnp.zeros(acc_s.shape, F32)

    def block_col(b):
        return jnp.sum(jnp.where(lane_s == b, sel, 0.0), axis=-1, keepdims=True)

    def picked(first_block):
        half = jnp.where(lane_k < SEL_BLOCK, block_col(first_block), block_col(first_block + 1))
        return jnp.concatenate([half] * NSA_HEADS, axis=0) > 0.5

    kvs = [_split_kv(p) for p in pages]
    last = g == pl.num_programs(1) - 1
    bias = [tb_ref[2]] * (PG - 1) + [tb_ref[jnp.where(last, 0, 2)]]
    sc = jnp.concatenate([_nt(q, k) for k, _ in kvs], axis=1) * scale + jnp.concatenate(bias, axis=1)
    keep = jnp.concatenate([picked(2 * (g * PG + j)) for j in range(PG)], axis=1)
    state = _softmax_step(m_s[...], l_s[...], acc_s[...], sc, keep, _by_block([v for _, v in kvs], _mm))
    m_s[...], l_s[...], acc_s[...] = state

    @pl.when(g == pl.num_programs(1) - 1)
    def _():
        new_ok = _new_key_mask(rows, n_new)
        kv = nslc_ref[...]
        sc = _nt(q, kv[:, 0:HD].astype(BF16)) * scale + tb_ref[1]
        _, l, acc = _softmax_step(*state, sc, picked(2 * n_pages) & new_ok, kv[:, HD:2 * HD].astype(BF16))
        o_slc = _finish(l, acc)

        wbuf = win_ref.shape[0] // 2
        k, v = _split_kv(win_ref)
        sc = _nt(q, k) * scale + wb_ref[...]
        t = lax.broadcasted_iota(jnp.int32, (rows, wbuf), 0) % TP
        n = lax.broadcasted_iota(jnp.int32, (rows, wbuf), 1)
        init = (jnp.full((rows, 1), NEG, F32), jnp.zeros((rows, 1), F32), jnp.zeros((rows, HD), F32))
        st = _softmax_step(*init, sc, wbuf + t - n <= WINDOW, v)
        kv = nwin_ref[...]
        sc = _nt(q, kv[:, 0:HD].astype(BF16)) * scale + tb_ref[1]
        _, l, acc = _softmax_step(*st, sc, new_ok, kv[:, HD:2 * HD].astype(BF16))
        o_win = _finish(l, acc)

        gate = small_ref[...]
        o_cmp = ocmp_ref[...]
        for h in range(NSA_HEADS):
            r = slice(TP * h, TP * (h + 1))
            o_ref[:, HD * h:HD * (h + 1)] = (gate[:, 3 * h:3 * h + 1] * o_cmp[r] + gate[:, 3 * h + 1:3 * h + 2] * o_slc[r]
                                             + gate[:, 3 * h + 2:3 * h + 3] * o_win[r])


def _s_mla_kernel(pt_ref, q_ref, *refs, n_new):
    pages = refs[:PG]
    (new_ref, wuv_ref, o_ref, m_s, l_s, acc_s) = refs[PG:]
    g = pl.program_id(1)
    rows = MLA_HEADS * TP
    scale = MLA_QK ** -0.5
    n_qk = KV_LORA + MLA_ROPE
    q = q_ref[:, 0:n_qk]

    @pl.when(g == 0)
    def _():
        m_s[...] = jnp.full(m_s.shape, NEG, F32)
        l_s[...] = jnp.zeros(l_s.shape, F32)
        acc_s[...] = jnp.zeros(acc_s.shape, F32)

    def update(state, page_refs, mask):
        kts = [r[0:n_qk, :].astype(BF16) for r in page_refs]
        inv = jnp.concatenate([r[n_qk:MLA_ROW, :] for r in page_refs], axis=1)
        mult = jnp.concatenate([jnp.broadcast_to(inv[h:h + 1], (TP, inv.shape[1])) for h in range(MLA_HEADS)], axis=0)
        sc = jnp.concatenate([_mm(q, kt) for kt in kts], axis=1) * (mult * scale)
        return _softmax_step(*state, sc, mask, _by_block([kt[0:KV_LORA] for kt in kts], _nt))

    state = update((m_s[...], l_s[...], acc_s[...]), pages, None)
    m_s[...], l_s[...], acc_s[...] = state

    @pl.when(g == pl.num_programs(1) - 1)
    def _():
        _, l, acc = update(state, [new_ref], _new_key_mask(rows, n_new))
        o_lat = _finish(l, acc).astype(BF16)
        for h in range(MLA_HEADS):
            o_ref[:, HD * h:HD * (h + 1)] = _mm(o_lat[TP * h:TP * (h + 1)], wuv_ref[h])


def _s_dsa_kernel(pt_ref, qidx_ref, wb_ref, qc_ref, *refs, n_pages, n_new, ksel):
    kpages = refs[:PG]
    vpages = refs[PG:2 * PG]
    (nkidx_ref, nkv_ref, tb_ref, o_ref, key_s, thr_s, m_s, l_s, acc_s) = refs[2 * PG:]
    g = pl.program_id(1)
    ng = n_pages // PG
    rows = DSA_HEADS * TP
    scale = HD ** -0.5

    def idx_scores(kt):
        r = jnp.maximum(_mm(qidx_ref[...], kt.astype(BF16)), 0.0) * wb_ref[...]
        sc = r[0:TP]
        for h in range(1, IDX_HEADS):
            sc = sc + r[TP * h:TP * (h + 1)]
        return sc

    @pl.when(g == 0)
    def _():
        m_s[...] = jnp.full(m_s.shape, NEG, F32)
        l_s[...] = jnp.zeros(l_s.shape, F32)
        acc_s[...] = jnp.zeros(acc_s.shape, F32)

    @pl.when(g < ng)
    def _():
        for j in range(PG):
            key_s[g * PG + j] = _order_key(idx_scores(kpages[j][...]))

    @pl.when(g == ng - 1)
    def _():
        sc = jnp.where(_new_key_mask(TP, n_new), idx_scores(nkidx_ref[...]), -jnp.inf)
        key_s[n_pages] = _order_key(sc)

        def count_ge(cand):
            c = jnp.sum((key_s[...] >= cand[None]).astype(jnp.int32), axis=0)
            return jnp.sum(c, axis=-1, keepdims=True)

        thr_s[...] = jnp.broadcast_to(_kth_threshold(count_ge, TP, ksel), thr_s.shape)

    def kept(pidx):
        return jnp.concatenate([key_s[pidx] >= thr_s[...]] * DSA_HEADS, axis=0)

    @pl.when(g >= ng)
    def _():
        kvs = [_split_kv(p) for p in vpages]
        bias = [tb_ref[2]] * (PG - 1) + [tb_ref[jnp.where(g == 2 * ng - 1, 0, 2)]]
        sc = jnp.concatenate([_nt(qc_ref[...], k) for k, _ in kvs], axis=1) * scale + jnp.concatenate(bias, axis=1)
        keep = jnp.concatenate([kept((g - ng) * PG + j) for j in range(PG)], axis=1)
        state = _softmax_step(m_s[...], l_s[...], acc_s[...], sc, keep, _by_block([v for _, v in kvs], _mm))
        m_s[...], l_s[...], acc_s[...] = state

    @pl.when(g == 2 * ng - 1)
    def _():
        kv = nkv_ref[...]
        sc = _nt(qc_ref[...], kv[:, 0:HD].astype(BF16)) * scale + tb_ref[1]
        _, l, acc = _softmax_step(m_s[...], l_s[...], acc_s[...], sc, kept(n_pages) & _new_key_mask(rows, n_new),
                                  kv[:, HD:2 * HD].astype(BF16))
        o = _finish(l, acc)
        for h in range(DSA_HEADS):
            o_ref[:, HD * h:HD * (h + 1)] = o[TP * h:TP * (h + 1)]


def _paged_call(body, page_table, ins, in_specs, out_shapes, out_specs, scratch, n_steps):
    nb = page_table.shape[0]
    return pl.pallas_call(
        body,
        grid_spec=pltpu.PrefetchScalarGridSpec(
            num_scalar_prefetch=1, grid=(nb, n_steps), in_specs=in_specs, out_specs=out_specs,
            scratch_shapes=scratch),
        out_shape=out_shapes,
        compiler_params=_cparams(("parallel", "arbitrary")),
        name=body.func.__name__.strip("_"),
    )(page_table, *ins)


def _softmax_scratch(rows, width):
    return [pltpu.VMEM((rows, 1), F32), pltpu.VMEM((rows, 1), F32), pltpu.VMEM((rows, width), F32)]


def _sample_attention(layer, page_table, n_new, q_a, q_c, q_idx, q_mla, small, w_idx, new, pools, state_win, tabs, wts):
    nb, n_pages = page_table.shape
    ng = n_pages // PG
    past = n_pages * PAGE
    rows4 = NSA_HEADS * TP
    nj = tabs["cb_s"].shape[1]
    nsp = tabs["cover_s"].shape[1]
    kv_page = (2 * PAGE, HD)
    kv_new = (PAGE, 2 * HD)

    ins = [q_a] + [pools["cmp"]] * PG + [new["cmp"], wts["w3"], wts["wn"], wts["g_cmp"], tabs["cb_s"], tabs["cover_s"]]
    specs = ([_seq_spec((rows4, HD))] + _page_specs(layer, kv_page, ng)
             + [_seq_spec((TP, 2 * HD))] + [_const_spec(a) for a in ins[PG + 2:]])
    o_cmp, sel = _paged_call(
        functools.partial(_s_cmp_kernel, n_pages=n_pages, n_new=n_new, n_pick=min(N_SEL, 2 * n_pages + 1)),
        page_table, ins, specs,
        [jax.ShapeDtypeStruct((nb, rows4, HD), F32), jax.ShapeDtypeStruct((nb, TP, nsp), F32)],
        [_seq_spec((rows4, HD)), _seq_spec((TP, nsp))],
        [pltpu.VMEM((nj, HD), F32), pltpu.VMEM((nj, HD), F32), pltpu.VMEM((8, HD), F32)], ng)

    ins = ([q_a, sel, o_cmp, small] + [pools["slc"]] * PG
           + [new["slc"], state_win, new["win"], tabs["tbs_a"], tabs["wb"]])
    specs = ([_seq_spec((rows4, HD)), _seq_spec((TP, nsp)), _seq_spec((rows4, HD)), _seq_spec((TP, LANE))]
             + _page_specs(layer, kv_page, ng)
             + [_seq_spec(kv_new),
                pl.BlockSpec((None, None) + state_win.shape[2:], lambda b, g, pt: (layer, b, 0, 0)),
                _seq_spec(kv_new), _const_spec(tabs["tbs_a"]), _const_spec(tabs["wb"])])
    o_a = _paged_call(
        functools.partial(_s_slc_kernel, n_pages=n_pages, n_new=n_new),
        page_table, ins, specs, jax.ShapeDtypeStruct((nb, TP, NSA_HEADS * HD), F32),
        _seq_spec((TP, NSA_HEADS * HD)), _softmax_scratch(rows4, HD), ng)

    rows8 = MLA_HEADS * TP
    ins = [q_mla] + [pools["mla"]] * PG + [new["mla"], wts["wuv"]]
    specs = ([_seq_spec((rows8, ROWP))] + _page_specs(layer, (MLA_ROW, PAGE), ng)
             + [_seq_spec((MLA_ROW, PAGE)), _const_spec(wts["wuv"])])
    o_b = _paged_call(
        functools.partial(_s_mla_kernel, n_new=n_new),
        page_table, ins, specs, jax.ShapeDtypeStruct((nb, TP, MLA_HEADS * HD), F32),
        _seq_spec((TP, MLA_HEADS * HD)), _softmax_scratch(rows8, KV_LORA), ng)

    rows16 = IDX_HEADS * TP
    ins = ([q_idx, w_idx, q_c] + [pools["kidx"]] * PG + [pools["dkv"]] * PG
           + [new["kidx"], new["dkv"], tabs["tbs_c"]])
    specs = ([_seq_spec((rows16, IDX_DIM)), _seq_spec((rows16, PAGE)), _seq_spec((rows4, HD))]
             + _page_specs(layer, (IDX_DIM, PAGE), ng) + _page_specs(layer, kv_page, ng, first_group=ng)
             + [_seq_spec((IDX_DIM, PAGE)), _seq_spec(kv_new), _const_spec(tabs["tbs_c"])])
    o_c = _paged_call(
        functools.partial(_s_dsa_kernel, n_pages=n_pages, n_new=n_new, ksel=min(IDX_TOPK, (past + n_new) // 4)),
        page_table, ins, specs, jax.ShapeDtypeStruct((nb, TP, DSA_HEADS * HD), F32),
        _seq_spec((TP, DSA_HEADS * HD)),
        [pltpu.VMEM((n_pages + 1, TP, PAGE), jnp.int32), pltpu.VMEM((TP, PAGE), jnp.int32)]
        + _softmax_scratch(rows4, HD), 2 * ng)
    return o_a, o_b, o_c


def _t5_bucket(n):
    max_exact = NUM_BUCKETS // 2
    nf = jnp.maximum(n, 1).astype(F32)
    large = max_exact + (jnp.log(nf / max_exact) / math.log(MAX_DISTANCE / max_exact)
                         * (NUM_BUCKETS - max_exact)).astype(jnp.int32)
    return jnp.where(n < max_exact, n, jnp.minimum(large, NUM_BUCKETS - 1))


def _round_up(x, m):
    return -(-x // m) * m


def _cover(n_rows, n_cols, first_row, n_valid, n_blocks):
    n = np.arange(n_rows)[:, None] - first_row
    s = np.arange(n_cols)[None]
    shared = np.maximum(np.minimum(CMP_STRIDE * n + CMP_LEN, SEL_BLOCK * s + SEL_BLOCK)
                        - np.maximum(CMP_STRIDE * n, SEL_BLOCK * s), 0)
    ok = (n >= 0) & (n < n_valid) & (s < n_blocks)
    return jnp.asarray(np.where(ok, shared / CMP_LEN, 0.0), F32)


def _tables(rel_bias, s, n_pages, n_new, wbuf):
    nq = s // QT
    past = n_pages * PAGE
    bd = rel_bias[_t5_bucket(jnp.arange(BIAS_CLIP + 1))].T.astype(F32)

    def look(heads, dist):
        return jnp.take(bd[heads], jnp.clip(dist, 0, BIAS_CLIP), axis=1)

    nsa, dsa = slice(0, NSA_HEADS), slice(NSA_HEADS, NSA_HEADS + DSA_HEADS)
    t = jnp.arange(QT)[:, None]
    c = jnp.arange(QT)[None]
    tb = lambda heads: jnp.stack([look(heads, QT * d + t - c) for d in range(3)]).reshape(3, -1, QT)
    ncp = s // CMP_STRIDE
    cb_p = look(nsa, jnp.arange(s)[:, None] - CMP_STRIDE * jnp.arange(ncp)[None] - (CMP_LEN - 1))
    cb_p = cb_p.reshape(NSA_HEADS, nq, QT, ncp).transpose(1, 0, 2, 3).reshape(nq, NSA_HEADS * QT, ncp)

    t8 = jnp.arange(TP)[:, None]
    n = jnp.arange(PAGE)[None]
    tbs = lambda heads: jnp.stack([look(heads, PAGE + t8 - n), look(heads, t8 - n),
                                   look(heads, jnp.full((TP, PAGE), BIAS_CLIP))]).reshape(3, -1, PAGE)
    nj = _round_up(8 * n_pages + 8, LANE)
    n_sel_blocks = -(-(past + n_new) // SEL_BLOCK)
    etab = (np.arange(s // SEL_BLOCK)[None, :, None]
            == (np.arange(nq)[:, None, None] * QT + np.arange(QT)[None, None, :]) // SEL_BLOCK)
    return dict(
        tb_a=tb(nsa), tb_c=tb(dsa), cb_p=cb_p,
        cover_p=_cover(ncp, s // SEL_BLOCK, 0, ncp - 1, s // SEL_BLOCK),
        etab=jnp.asarray(etab, BF16),
        tbs_a=tbs(nsa), tbs_c=tbs(dsa),
        wb=look(nsa, wbuf + t8 - jnp.arange(wbuf)[None]).reshape(-1, wbuf),
        cb_s=look(nsa, past + t8 - CMP_STRIDE * (jnp.arange(nj)[None] - 1) - (CMP_LEN - 1)).reshape(-1, nj),
        cover_s=_cover(nj, _round_up(n_sel_blocks, LANE), 1, 8 * n_pages, n_sel_blocks),
    )


_PROJ_SRC = np.concatenate([
    np.arange(0, 512), np.arange(512, 1280),
    np.arange(2124, 2636), np.arange(2636, 2892),
    np.arange(2892, 3916),
    np.arange(1292, 1804), np.arange(1804, 2060),
    np.arange(2060, 2124), np.arange(3916, 3980),
    np.arange(1280, 1292), np.arange(3980, 3996), np.full(LANE - GATE_W - IDX_HEADS, -1),
])


def _layer_weights(l, w_in, w_nsa_cmp, g_nsa_qk, g_mla_qlat, g_mla_kvlat, w_mla_uq, w_mla_uk, w_mla_uv,
                   g_mla_q, g_mla_k, g_dsa_qk, s):
    src = jnp.asarray(np.maximum(_PROJ_SRC, 0))
    w_in_p = jnp.where(jnp.asarray(_PROJ_SRC >= 0)[None], jnp.take(w_in[l], src, axis=1), 0.0).astype(BF16)

    def row(v):
        return jnp.pad(v.astype(F32), (0, Q_LORA - v.shape[0]))
    zeros = jnp.zeros((LANE - MLA_ROPE,), F32)
    gpack = jnp.stack([
        row(g_nsa_qk[l, 0]), row(g_nsa_qk[l, 2]), row(g_nsa_qk[l, 3]), row(g_dsa_qk[l, 0]), row(g_dsa_qk[l, 1]),
        row(g_mla_qlat[l]), row(g_mla_kvlat[l]), row(jnp.concatenate([g_mla_q[l], zeros])),
        row(g_mla_k[l, :MLA_NOPE]), row(jnp.concatenate([g_mla_k[l, MLA_NOPE:], zeros]))]
        + [jnp.zeros((Q_LORA,), F32)] * 6)

    wc = w_nsa_cmp[l].astype(F32)
    ncp = s // CMP_STRIDE
    off = np.arange(s)[None] - CMP_STRIDE * np.arange(ncp)[:, None]
    ok = (off >= 0) & (off < CMP_LEN) & (np.arange(ncp)[:, None] < ncp - 1)
    band = lambda k: jnp.where(jnp.asarray(ok), wc[k][jnp.asarray(np.clip(off, 0, CMP_LEN - 1))], 0.0).astype(BF16)
    r = np.arange(PAGE)[None]
    cc = np.arange(16)[:, None]
    tail_off = r - CMP_STRIDE * cc
    head_off = r - CMP_STRIDE * (cc - 1)
    tail_ok = (cc < 8) & (tail_off >= 0) & (tail_off < CMP_STRIDE)
    head_ok = (cc >= 1) & (cc <= 8) & (head_off >= 0) & (head_off < CMP_STRIDE)
    page_w = lambda k: (jnp.where(jnp.asarray(tail_ok), wc[k][jnp.asarray(np.clip(tail_off, 0, 15) + CMP_STRIDE)], 0.0)
                        + jnp.where(jnp.asarray(head_ok), wc[k][jnp.asarray(np.clip(head_off, 0, 15))], 0.0))
    zero = jnp.zeros((16, PAGE), F32)
    w3 = jnp.concatenate([jnp.stack([page_w(0), zero], axis=-1).reshape(16, 2 * PAGE),
                          jnp.stack([zero, page_w(1)], axis=-1).reshape(16, 2 * PAGE)]).astype(BF16)
    wn = jnp.broadcast_to(wc[:, CMP_STRIDE:CMP_STRIDE + TP, None], (2, TP, HD))

    return dict(
        w_in=w_in_p, gpack=gpack,
        wuq2=jnp.pad(w_mla_uq[l], ((0, 0), (0, 0), (0, LANE - MLA_ROPE))).reshape(Q_LORA, -1).astype(BF16),
        wukt=w_mla_uk[l].transpose(1, 2, 0).astype(BF16),
        wukf=w_mla_uk[l].reshape(KV_LORA, -1).astype(BF16),
        wuv=w_mla_uv[l].transpose(1, 0, 2).astype(BF16),
        band_k=band(0), band_v=band(1), w3=w3, wn=wn,
        g_cmp=g_nsa_qk[l, 1].reshape(1, HD).astype(F32),
    )


def kernel(x_prompt, x_sample, cache_nsa_cmp_kv, cache_nsa_slc_kv, state_nsa_win_kv, cache_mla, cache_dsa_kv, cache_dsa_kidx, page_table, rel_bias, g_mix, w_in, g_nsa_qk, w_nsa_cmp, g_mla_qlat, g_mla_kvlat, w_mla_uq, w_mla_uk, w_mla_uv, g_mla_q, g_mla_k, g_dsa_qk, w_out, g_ffn, w_up, w_down):
    nb, s, _ = x_prompt.shape
    nd, n_new, _ = x_sample.shape
    depth, n_pool = cache_mla.shape[:2]
    n_pages = page_table.shape[1]
    past = n_pages * PAGE
    wbuf = state_nsa_win_kv.shape[2]
    assert s % (KC * QT) == 0 and n_pages % PG == 0 and n_new <= TP and wbuf >= WINDOW
    mp, ms = nb * s, nd * n_new
    m_pad = _round_up(mp + ms, MM_T)
    pad_rows = lambda a: jnp.pad(a, ((0, m_pad - a.shape[0]), (0, 0)))
    smp = slice(mp, mp + ms)

    x = pad_rows(jnp.concatenate([x_prompt.reshape(mp, D_MODEL), x_sample.reshape(ms, D_MODEL)]))
    pos = jnp.concatenate([jnp.tile(jnp.arange(s), nb), jnp.tile(past + jnp.arange(n_new), nd)]).astype(F32)
    half = MLA_ROPE // 2
    ang = pos[:, None] * (ROPE_THETA ** (-jnp.arange(half, dtype=F32) / half))[None]
    cos, sin, z = jnp.cos(ang), jnp.sin(ang), jnp.zeros((mp + ms, LANE - MLA_ROPE), F32)
    cs = pad_rows(jnp.concatenate([cos, cos, z, -sin, sin, z], axis=1))

    tabs = _tables(rel_bias, s, n_pages, n_new, wbuf)
    pools = dict(
        cmp=cache_nsa_cmp_kv.reshape(depth, n_pool, 2 * PAGE, HD),
        slc=cache_nsa_slc_kv.reshape(depth, n_pool, 2 * PAGE, HD),
        dkv=cache_dsa_kv.reshape(depth, n_pool, 2 * PAGE, HD),
        kidx=jnp.swapaxes(cache_dsa_kidx, -1, -2), mla=jnp.swapaxes(cache_mla, -1, -2))
    state_win = state_nsa_win_kv.reshape(depth, nd, 2 * wbuf, HD)

    def per_seq(a, rows):
        a = a.reshape(nd, n_new, a.shape[-1])
        return jnp.pad(a, ((0, 0), (0, rows - n_new), (0, 0)))

    def heads_first(a, n_heads):
        a = per_seq(a, TP).reshape(nd, TP, n_heads, -1)
        return a.transpose(0, 2, 1, 3).reshape(nd, n_heads * TP, -1)

    prompt_new, sample_new = [], []
    for l in range(depth):
        wts = _layer_weights(l, w_in, w_nsa_cmp, g_nsa_qk, g_mla_qlat, g_mla_kvlat, w_mla_uq, w_mla_uk,
                             w_mla_uv, g_mla_q, g_mla_k, g_dsa_qk, s)
        proj = _mm_full(x, wts["w_in"], gain=g_mix[l])
        qa, slc, win, kvc, qc, qidx, small, qmla, rowp = _post(
            proj, cs, wts["gpack"], wts["wuq2"], wts["wukt"], wts["wukf"])
        kvcmp = proj[:, C_CMP:C_CMP + 2 * HD]
        kidx = proj[:, C_A + MLA_ROPE:C_A + LANE]
        inv_r = rowp[:, KV_LORA + MLA_ROPE:MLA_ROW]

        ck, cv = _compress_prompt(kvcmp, wts["band_k"], wts["band_v"], wts["g_cmp"], nb, s)
        o_a = _nsa_prompt(qa, small, ck, cv, slc, win, tabs["tb_a"], tabs["cb_p"], tabs["cover_p"],
                          tabs["etab"], nb, s)
        inv_p = jnp.swapaxes(inv_r[:mp].reshape(nb, s // QT, QT, MLA_HEADS), -1, -2)
        o_b = _mla_prompt(qmla, rowp, inv_p, wts["wuv"], nb, s)
        kidx2 = jnp.concatenate([kidx[:mp], kidx[:mp]], axis=1).reshape(nb, s, LANE).astype(BF16)
        o_c = _dsa_prompt(qc, qidx, small, kidx2, kvc, tabs["tb_c"], nb, s)

        small_s = per_seq(small[smp], TP)
        w_idx = small_s[:, :, GATE_W:GATE_W + IDX_HEADS].transpose(0, 2, 1).reshape(nd, IDX_HEADS * TP, 1)
        new = dict(cmp=per_seq(kvcmp[smp], TP), slc=per_seq(slc[smp], PAGE), win=per_seq(win[smp], PAGE),
                   mla=jnp.swapaxes(per_seq(rowp[smp, :MLA_ROW], PAGE), -1, -2),
                   kidx=jnp.swapaxes(per_seq(kidx[smp], PAGE), -1, -2), dkv=per_seq(kvc[smp], PAGE))
        so_a, so_b, so_c = _sample_attention(
            l, page_table, n_new, heads_first(qa[smp], NSA_HEADS), heads_first(qc[smp], DSA_HEADS),
            heads_first(qidx[smp], IDX_HEADS), heads_first(qmla[smp], MLA_HEADS), small_s,
            jnp.broadcast_to(w_idx, (nd, IDX_HEADS * TP, PAGE)), new, pools, state_win, tabs, wts)

        mixed = pad_rows(jnp.concatenate([
            jnp.concatenate([o_a, o_b, o_c], axis=1),
            jnp.concatenate([so_a[:, :n_new], so_b[:, :n_new], so_c[:, :n_new]], axis=2).reshape(ms, -1)]))
        x1 = _mm_full(mixed, w_out[l].astype(BF16), res=x)
        hidden = _mm_full(x1, w_up[l].astype(BF16), gain=g_ffn[l], act="relu2", out_dtype=BF16)
        x = _mm_ktiled(hidden, w_down[l].astype(BF16), x1)

        kv4 = lambda a, r, n: a[r].reshape(n, -1, 2, HD)
        pr = slice(0, mp)
        prompt_new.append((kv4(kvcmp, pr, nb), kv4(slc, pr, nb), kv4(win, pr, nb)[:, -min(WINDOW, s):],
                           rowp[pr, :MLA_ROW].reshape(nb, s, MLA_ROW), kv4(kvc, pr, nb),
                           kidx[pr].reshape(nb, s, IDX_DIM)))
        win_all = jnp.concatenate([state_nsa_win_kv[l], kv4(win, smp, nd)], axis=1)
        sample_new.append((kv4(kvcmp, smp, nd), kv4(slc, smp, nd), win_all[:, -min(WINDOW, past + n_new):],
                           rowp[smp, :MLA_ROW].reshape(nd, n_new, MLA_ROW), kv4(kvc, smp, nd),
                           kidx[smp].reshape(nd, n_new, IDX_DIM)))

    outs_p = [jnp.stack(a) for a in zip(*prompt_new)]
    outs_s = [jnp.stack(a) for a in zip(*sample_new)]
    return (x[:mp].reshape(nb, s, D_MODEL), x[smp].reshape(nd, n_new, D_MODEL), *outs_p, *outs_s)
```

```python
import functools
import math

import numpy as np
import jax
import jax.numpy as jnp
from jax import lax
from jax.experimental import pallas as pl
from jax.experimental.pallas import tpu as pltpu

F32 = jnp.float32
BF16 = jnp.bfloat16

D_MODEL = 2048
PAGE = 128
NSA_HEADS = 4
HD = 128
CMP_LEN = 32
CMP_STRIDE = 16
SEL_BLOCK = 64
N_SEL = 16
WINDOW = 512
MLA_HEADS = 8
MLA_NOPE = 128
MLA_ROPE = 64
MLA_QK = MLA_NOPE + MLA_ROPE
Q_LORA = 512
KV_LORA = 256
MLA_ROW = KV_LORA + MLA_ROPE + MLA_HEADS
DSA_HEADS = 4
IDX_HEADS = 16
IDX_DIM = 64
IDX_TOPK = 256
NUM_BUCKETS = 32
MAX_DISTANCE = 128
ROPE_THETA = 10000.0
EPS = 1e-6
NEG = -1e30
FORCE_BONUS = 1e4

LANE = 128
QT = 128
KC = 4
TP = 8
PG = 16
ROWP = 384
MM_T = 512
MM_TK = 2048
POST_T = 256
VMEM_LIMIT = 56 * 1024 * 1024
FAR = 2 * MAX_DISTANCE

C_QA, C_CMP, C_SLC, C_WIN, C_QC, C_KVC, C_QIDX, C_DQ, C_DKV, C_A, C_B, PROJ_P = (
    0, 512, 768, 1024, 1280, 1792, 2048, 3072, 3584, 3840, 3968, 4096)
GATE_W = 3 * NSA_HEADS


def _nt(a, b):
    return lax.dot_general(a, b, (((1,), (1,)), ((), ())), preferred_element_type=F32)


def _mm(a, b):
    return jnp.dot(a, b, preferred_element_type=F32)


def _cparams(sem):
    return pltpu.CompilerParams(dimension_semantics=sem, vmem_limit_bytes=VMEM_LIMIT)


def _softmax_step(m, l, acc, s, mask, v):
    if mask is not None:
        s = jnp.where(mask, s, NEG)
    m_new = jnp.maximum(m, jnp.max(s, axis=-1, keepdims=True))
    alpha = jnp.exp(m - m_new)
    e = jnp.exp(s - m_new)
    if mask is not None:
        e = jnp.where(mask, e, 0.0)
    l = alpha * l + jnp.sum(e, axis=-1, keepdims=True)
    e = e.astype(BF16)
    acc = alpha * acc + (v(e) if callable(v) else _mm(e, v))
    return m_new, l, acc


def _by_block(blocks, fn):
    def pv(e):
        out = fn(e[:, 0:QT], blocks[0])
        for j in range(1, len(blocks)):
            out = out + fn(e[:, QT * j:QT * (j + 1)], blocks[j])
        return out
    return pv


def _finish(l, acc):
    return acc / jnp.maximum(l, 1e-30)


def _top_blocks(score, n_pick):
    lane = lax.broadcasted_iota(jnp.int32, score.shape, 1)
    width = score.shape[1]
    sel = jnp.zeros(score.shape, F32)
    for _ in range(n_pick):
        mx = jnp.max(score, axis=-1, keepdims=True)
        idx = jnp.min(jnp.where(score == mx, lane, width), axis=-1, keepdims=True)
        hit = lane == idx
        sel = jnp.where(hit, 1.0, sel)
        score = jnp.where(hit, -jnp.inf, score)
    return sel


def _order_key(x):
    bits = lax.bitcast_convert_type(x, jnp.int32)
    return jnp.where(bits < 0, bits ^ jnp.int32(0x7FFFFFFF), bits)


def _kth_threshold(count_ge, rows, k):
    def step(i, t):
        cand = t + lax.shift_left(jnp.int32(1), jnp.int32(31) - i)
        return jnp.where(count_ge(cand) >= k, cand, t)
    t0 = jnp.full((rows, 1), jnp.iinfo(jnp.int32).min, jnp.int32)
    return lax.fori_loop(0, 32, step, t0)


def _mm_full_kernel(*refs, has_gain, has_res, act):
    a_ref, w_ref = refs[0], refs[1]
    k = 2
    g_ref = r_ref = None
    if has_gain:
        g_ref = refs[k]; k += 1
    if has_res:
        r_ref = refs[k]; k += 1
    o_ref, abuf = refs[k], refs[k + 1]

    @pl.when(pl.program_id(1) == 0)
    def _():
        a = a_ref[...].astype(F32)
        if has_gain:
            ms = jnp.mean(a * a, axis=-1, keepdims=True)
            a = a * lax.rsqrt(ms + EPS) * g_ref[...]
        abuf[...] = a.astype(BF16)

    y = _mm(abuf[...], w_ref[...])
    if act == "relu2":
        y = jnp.square(jnp.maximum(y, 0.0))
    if has_res:
        y = y + r_ref[...]
    o_ref[...] = y.astype(o_ref.dtype)


def _mm_full(a, w, gain=None, res=None, act=None, out_dtype=F32):
    m, kdim = a.shape
    n = w.shape[1]
    ins = [a, w]
    specs = [pl.BlockSpec((MM_T, kdim), lambda i, j: (i, 0)),
             pl.BlockSpec((kdim, MM_T), lambda i, j: (0, j))]
    if gain is not None:
        ins.append(gain.reshape(1, kdim).astype(F32))
        specs.append(pl.BlockSpec((1, kdim), lambda i, j: (0, 0)))
    if res is not None:
        ins.append(res)
        specs.append(pl.BlockSpec((MM_T, MM_T), lambda i, j: (i, j)))
    return pl.pallas_call(
        functools.partial(_mm_full_kernel, has_gain=gain is not None, has_res=res is not None, act=act),
        grid=(m // MM_T, n // MM_T),
        in_specs=specs,
        out_specs=pl.BlockSpec((MM_T, MM_T), lambda i, j: (i, j)),
        out_shape=jax.ShapeDtypeStruct((m, n), out_dtype),
        scratch_shapes=[pltpu.VMEM((MM_T, kdim), BF16)],
        compiler_params=_cparams(("parallel", "arbitrary")),
        name="matmul_" + (act or ("norm" if gain is not None else "plain")),
    )(*ins)


def _mm_kt_kernel(a_ref, w_ref, r_ref, o_ref, acc):
    kk = pl.program_id(2)

    @pl.when(kk == 0)
    def _():
        acc[...] = r_ref[...]

    acc[...] += _mm(a_ref[...], w_ref[...])

    @pl.when(kk == pl.num_programs(2) - 1)
    def _():
        o_ref[...] = acc[...]


def _mm_ktiled(a, w, res):
    m, kdim = a.shape
    n = w.shape[1]
    return pl.pallas_call(
        _mm_kt_kernel,
        grid=(m // MM_T, n // MM_T, kdim // MM_TK),
        in_specs=[pl.BlockSpec((MM_T, MM_TK), lambda i, j, k: (i, k)),
                  pl.BlockSpec((MM_TK, MM_T), lambda i, j, k: (k, j)),
                  pl.BlockSpec((MM_T, MM_T), lambda i, j, k: (i, j))],
        out_specs=pl.BlockSpec((MM_T, MM_T), lambda i, j, k: (i, j)),
        out_shape=jax.ShapeDtypeStruct((m, n), F32),
        scratch_shapes=[pltpu.VMEM((MM_T, MM_T), F32)],
        compiler_params=_cparams(("parallel", "parallel", "arbitrary")),
        name="matmul_ktiled",
    )(a, w, res)


def _post_kernel(p_ref, cs_ref, g_ref, wuq_ref, wukt_ref, wukf_ref,
                 qa_ref, slc_ref, win_ref, kvc_ref, qc_ref, qidx_ref, small_ref, qmla_ref, row_ref):
    def nrm(x, g):
        ms = jnp.mean(x * x, axis=-1, keepdims=True)
        return x * lax.rsqrt(ms + EPS) * g

    def gain(r, w):
        return g_ref[r:r + 1, 0:w]

    for h in range(NSA_HEADS):
        qa_ref[:, HD * h:HD * (h + 1)] = nrm(p_ref[:, C_QA + HD * h:C_QA + HD * (h + 1)], gain(0, HD)).astype(BF16)
        qc_ref[:, HD * h:HD * (h + 1)] = nrm(p_ref[:, C_QC + HD * h:C_QC + HD * (h + 1)], gain(3, HD)).astype(BF16)
    for ref, col, r in ((slc_ref, C_SLC, 1), (win_ref, C_WIN, 2), (kvc_ref, C_KVC, 4)):
        ref[:, 0:HD] = nrm(p_ref[:, col:col + HD], gain(r, HD))
        ref[:, HD:2 * HD] = p_ref[:, col + HD:col + 2 * HD]
    qidx_ref[...] = (p_ref[:, C_QIDX:C_QIDX + IDX_HEADS * IDX_DIM] * IDX_DIM ** -0.5).astype(BF16)

    tm = p_ref.shape[0]
    lane = lax.broadcasted_iota(jnp.int32, (tm, LANE), 1)
    blk_b = p_ref[:, C_B:C_B + LANE]
    small_ref[...] = jnp.where(lane < GATE_W, jax.nn.sigmoid(blk_b), blk_b * IDX_HEADS ** -0.5)

    cos = cs_ref[:, 0:LANE]
    sin = cs_ref[:, LANE:2 * LANE]

    def rope(x):
        swapped = jnp.where(lane < MLA_ROPE // 2, pltpu.roll(x, LANE - MLA_ROPE // 2, 1),
                            pltpu.roll(x, MLA_ROPE // 2, 1))
        return x * cos + swapped * sin

    cq = nrm(p_ref[:, C_DQ:C_DQ + Q_LORA], gain(5, Q_LORA)).astype(BF16)
    q2 = _mm(cq, wuq_ref[...])
    for h in range(MLA_HEADS):
        blk = q2[:, 2 * LANE * h:2 * LANE * (h + 1)]
        ms = jnp.sum(blk * blk, axis=-1, keepdims=True) * (1.0 / MLA_QK)
        qn = blk * lax.rsqrt(ms + EPS) * gain(7, 2 * LANE)
        qlat = _mm((qn[:, 0:MLA_NOPE] * gain(8, MLA_NOPE)).astype(BF16), wukt_ref[h])
        qmla_ref[:, ROWP * h:ROWP * h + KV_LORA] = qlat.astype(BF16)
        qmla_ref[:, ROWP * h + KV_LORA:ROWP * (h + 1)] = rope(qn[:, MLA_NOPE:2 * LANE]).astype(BF16)

    ckv = nrm(p_ref[:, C_DKV:C_DKV + KV_LORA], gain(6, KV_LORA))
    kn = _mm(ckv.astype(BF16), wukf_ref[...])
    kpe = jnp.where(lane < MLA_ROPE, p_ref[:, C_A:C_A + LANE], 0.0)
    ssq_pe = jnp.sum(kpe * kpe, axis=-1, keepdims=True)
    third = rope(kpe * gain(9, LANE))
    for h in range(MLA_HEADS):
        kh = kn[:, MLA_NOPE * h:MLA_NOPE * (h + 1)]
        ms = (jnp.sum(kh * kh, axis=-1, keepdims=True) + ssq_pe) * (1.0 / MLA_QK)
        third = jnp.where(lane == MLA_ROPE + h, lax.rsqrt(ms + EPS), third)
    row_ref[:, 0:KV_LORA] = ckv
    row_ref[:, KV_LORA:ROWP] = third


def _post(proj, cs, gpack, wuq2, wukt, wukf):
    m = proj.shape[0]
    row = lambda w: pl.BlockSpec((POST_T, w), lambda i: (i, 0))
    full = lambda a: pl.BlockSpec(a.shape, lambda i: (0,) * a.ndim)
    widths = (4 * HD, 2 * HD, 2 * HD, 2 * HD, 4 * HD, IDX_HEADS * IDX_DIM, LANE, MLA_HEADS * ROWP, ROWP)
    dtypes = (BF16, F32, F32, F32, BF16, BF16, F32, BF16, F32)
    return pl.pallas_call(
        _post_kernel,
        grid=(m // POST_T,),
        in_specs=[row(PROJ_P), row(2 * LANE), full(gpack), full(wuq2), full(wukt), full(wukf)],
        out_specs=[row(w) for w in widths],
        out_shape=[jax.ShapeDtypeStruct((m, w), d) for w, d in zip(widths, dtypes)],
        compiler_params=_cparams(("parallel",)),
        name="post_projection",
    )(proj, cs, gpack, wuq2, wukt, wukf)


def _compress_kernel(kv_ref, wk_ref, wv_ref, g_ref, ck_ref, cv_ref):
    ck = _mm(wk_ref[...], kv_ref[:, 0:HD].astype(BF16))
    ms = jnp.mean(ck * ck, axis=-1, keepdims=True)
    ck_ref[...] = (ck * lax.rsqrt(ms + EPS) * g_ref[...]).astype(BF16)
    cv_ref[...] = _mm(wv_ref[...], kv_ref[:, HD:2 * HD].astype(BF16)).astype(BF16)


def _compress_prompt(kvcmp, wk, wv, g, nb, s):
    ncp = wk.shape[0]
    return pl.pallas_call(
        _compress_kernel,
        grid=(nb,),
        in_specs=[pl.BlockSpec((s, 2 * HD), lambda b: (b, 0)),
                  pl.BlockSpec(wk.shape, lambda b: (0, 0)),
                  pl.BlockSpec(wv.shape, lambda b: (0, 0)),
                  pl.BlockSpec((1, HD), lambda b: (0, 0))],
        out_specs=[pl.BlockSpec((None, ncp, HD), lambda b: (b, 0, 0))] * 2,
        out_shape=[jax.ShapeDtypeStruct((nb, ncp, HD), BF16)] * 2,
        compiler_params=_cparams(("parallel",)),
        name="compress_prompt",
    )(kvcmp, wk, wv, g)


def _tile_dist(qi, kc, rows):
    t = lax.broadcasted_iota(jnp.int32, (rows, QT), 0) % QT
    c = lax.broadcasted_iota(jnp.int32, (rows, QT), 1)
    return QT * (qi - kc) + t - c


def _nsa_prompt_kernel(qa_ref, small_ref, ck_ref, cv_ref, slc_ref, win_ref, tb_ref, cb_ref, cover_ref,
                       e_ref, o_ref, *, n_cmp, n_pick):
    qi = pl.program_id(1)
    rows = NSA_HEADS * QT
    scale = HD ** -0.5
    q4 = jnp.concatenate([qa_ref[:, HD * h:HD * (h + 1)] for h in range(NSA_HEADS)], axis=0)

    ncp = ck_ref.shape[0]
    s = _nt(q4, ck_ref[...]) * scale + cb_ref[...]
    t = lax.broadcasted_iota(jnp.int32, (rows, ncp), 0) % QT
    ci = lax.broadcasted_iota(jnp.int32, (rows, ncp), 1)
    mask = (QT * qi + t - CMP_STRIDE * ci - (CMP_LEN - 1) >= 0) & (ci < n_cmp)
    s = jnp.where(mask, s, NEG)
    e = jnp.where(mask, jnp.exp(s - jnp.max(s, axis=-1, keepdims=True)), 0.0)
    p = e / jnp.maximum(jnp.sum(e, axis=-1, keepdims=True), 1e-30)
    o_cmp = _mm(p.astype(BF16), cv_ref[...])
    psum = p[0:QT] + p[QT:2 * QT] + p[2 * QT:3 * QT] + p[3 * QT:4 * QT]
    imp = jnp.dot(psum, cover_ref[...], preferred_element_type=F32, precision=lax.Precision.HIGHEST)
    ns = imp.shape[1]
    blk = lax.broadcasted_iota(jnp.int32, (QT, ns), 1)
    qb = (QT * qi + lax.broadcasted_iota(jnp.int32, (QT, ns), 0)) // SEL_BLOCK
    forced = (blk == 0) | (blk == qb) | (blk == qb - 1)
    score = jnp.where(blk <= qb, imp + jnp.where(forced, FORCE_BONUS, 0.0), -jnp.inf)
    sel = _top_blocks(score, n_pick).astype(BF16)

    init = (jnp.full((rows, 1), NEG, F32), jnp.zeros((rows, 1), F32), jnp.zeros((rows, HD), F32))

    def logits(ref, kcs):
        ks, vs = [], []
        for kc in kcs:
            kv = ref[pl.ds(pl.multiple_of(kc * QT, QT), QT), :]
            ks.append(kv[:, 0:HD].astype(BF16))
            vs.append(kv[:, HD:2 * HD].astype(BF16))
        sc = jnp.concatenate([_nt(q4, k) for k in ks], axis=1) * scale
        bias = jnp.concatenate([tb_ref[jnp.clip(qi - kc, 0, 2)] for kc in kcs], axis=1)
        return sc + bias, vs

    def slc_body(i, carry):
        kcs = [i * KC + u for u in range(KC)]
        sc, vs = logits(slc_ref, kcs)
        picked = jnp.concatenate([_mm(sel, e_ref[kc]) for kc in kcs], axis=1)
        picked = jnp.concatenate([picked] * NSA_HEADS, axis=0)
        dist = jnp.concatenate([_tile_dist(qi, kc, rows) for kc in kcs], axis=1)
        return _softmax_step(*carry, sc, (picked > 0.5) & (dist >= 0), _by_block(vs, _mm))

    _, l, acc = lax.fori_loop(0, qi // KC + 1, slc_body, init)
    o_slc = _finish(l, acc)

    first = qi - WINDOW // QT
    kcs = [first + u for u in range(WINDOW // QT + 1)]
    sc, vs = logits(win_ref, [jnp.maximum(kc, 0) for kc in kcs])
    ok = []
    for kc in kcs:
        dist = _tile_dist(qi, kc, rows)
        ok.append((dist >= 0) & (dist <= WINDOW) & (kc >= 0))
    _, l, acc = _softmax_step(*init, sc, jnp.concatenate(ok, axis=1), _by_block(vs, _mm))
    o_win = _finish(l, acc)

    gate = small_ref[...]
    for h in range(NSA_HEADS):
        r = slice(QT * h, QT * (h + 1))
        o_ref[:, HD * h:HD * (h + 1)] = (gate[:, 3 * h:3 * h + 1] * o_cmp[r] + gate[:, 3 * h + 1:3 * h + 2] * o_slc[r]
                                         + gate[:, 3 * h + 2:3 * h + 3] * o_win[r])


def _nsa_prompt(qa, small, ck, cv, slc, win, tb, cb, cover, etab, nb, s):
    nq = s // QT
    ncp = ck.shape[1]
    rows = NSA_HEADS * QT
    tile = lambda w: pl.BlockSpec((QT, w), lambda b, i: (b * nq + i, 0))
    seq = lambda w: pl.BlockSpec((s, w), lambda b, i: (b, 0))
    const = lambda a: pl.BlockSpec(a.shape, lambda b, i: (0,) * a.ndim)
    return pl.pallas_call(
        functools.partial(_nsa_prompt_kernel, n_cmp=s // CMP_STRIDE - 1, n_pick=min(N_SEL, s // SEL_BLOCK)),
        grid=(nb, nq),
        in_specs=[tile(4 * HD), tile(LANE),
                  pl.BlockSpec((None, ncp, HD), lambda b, i: (b, 0, 0)),
                  pl.BlockSpec((None, ncp, HD), lambda b, i: (b, 0, 0)),
                  seq(2 * HD), seq(2 * HD), const(tb),
                  pl.BlockSpec((None, rows, ncp), lambda b, i: (i, 0, 0)),
                  const(cover), const(etab)],
        out_specs=pl.BlockSpec((QT, 4 * HD), lambda b, i: (b * nq + i, 0)),
        out_shape=jax.ShapeDtypeStruct((nb * s, 4 * HD), F32),
        compiler_params=_cparams(("parallel", "arbitrary")),
        name="nsa_prompt",
    )(qa, small, ck, cv, slc, win, tb, cb, cover, etab)


def _mla_prompt_kernel(q_ref, row_ref, inv_ref, wuv_ref, o_ref, m_s, l_s, acc_s):
    qi = pl.program_id(1)
    rows = MLA_HEADS * QT
    scale = MLA_QK ** -0.5
    q8 = jnp.concatenate([q_ref[:, ROWP * h:ROWP * (h + 1)] for h in range(MLA_HEADS)], axis=0)
    m_s[...] = jnp.full((rows, 1), NEG, F32)
    l_s[...] = jnp.zeros((rows, 1), F32)
    acc_s[...] = jnp.zeros((rows, KV_LORA), F32)

    def step(i, causal):
        kcs = [i * KC + u for u in range(KC)]
        tiles = [row_ref[pl.ds(pl.multiple_of(kc * QT, QT), QT), :].astype(BF16) for kc in kcs]
        inv = jnp.concatenate([inv_ref[kc] for kc in kcs], axis=1)
        mult = jnp.concatenate([jnp.broadcast_to(inv[h:h + 1], (QT, KC * QT)) for h in range(MLA_HEADS)], axis=0)
        sc = jnp.concatenate([_nt(q8, t) for t in tiles], axis=1) * (mult * scale)
        mask = jnp.concatenate([_tile_dist(qi, kc, rows) for kc in kcs], axis=1) >= 0 if causal else None
        m_, l_, a_ = _softmax_step(m_s[...], l_s[...], acc_s[...], sc, mask,
                                   _by_block([t[:, 0:KV_LORA] for t in tiles], _mm))
        m_s[...] = m_
        l_s[...] = l_
        acc_s[...] = a_

    def body(i, _):
        step(i, False)
        return 0

    lax.fori_loop(0, qi // KC, body, 0)
    step(qi // KC, True)
    o_lat = _finish(l_s[...], acc_s[...]).astype(BF16)
    for h in range(MLA_HEADS):
        o_ref[:, HD * h:HD * (h + 1)] = _mm(o_lat[QT * h:QT * (h + 1)], wuv_ref[h])


def _mla_prompt(qmla, rowp, invt, wuv, nb, s):
    nq = s // QT
    rows = MLA_HEADS * QT
    return pl.pallas_call(
        _mla_prompt_kernel,
        grid=(nb, nq),
        in_specs=[pl.BlockSpec((QT, MLA_HEADS * ROWP), lambda b, i: (b * nq + i, 0)),
                  pl.BlockSpec((s, ROWP), lambda b, i: (b, 0)),
                  pl.BlockSpec((None, nq, MLA_HEADS, QT), lambda b, i: (b, 0, 0, 0)),
                  pl.BlockSpec(wuv.shape, lambda b, i: (0, 0, 0))],
        out_specs=pl.BlockSpec((QT, MLA_HEADS * HD), lambda b, i: (b * nq + i, 0)),
        out_shape=jax.ShapeDtypeStruct((nb * s, MLA_HEADS * HD), F32),
        scratch_shapes=[pltpu.VMEM((rows, 1), F32), pltpu.VMEM((rows, 1), F32), pltpu.VMEM((rows, KV_LORA), F32)],
        compiler_params=_cparams(("parallel", "arbitrary")),
        name="mla_prompt",
    )(qmla, rowp, invt, wuv)


def _dsa_prompt_kernel(qc_ref, qidx_ref, small_ref, kidx_ref, kvc_ref, tb_ref, o_ref, key_s, *, ksel):
    qi = pl.program_id(1)
    rows = DSA_HEADS * QT
    scale = HD ** -0.5
    lane = lax.broadcasted_iota(jnp.int32, (QT, LANE), 1)
    w = small_ref[...]

    heads, wcols = [], []
    for j in range(IDX_HEADS // 2):
        x = qidx_ref[:, LANE * j:LANE * (j + 1)]
        heads += [jnp.where(lane < IDX_DIM, x, 0).astype(BF16), jnp.where(lane >= IDX_DIM, x, 0).astype(BF16)]
        wcols += [w[:, GATE_W + 2 * j:GATE_W + 2 * j + 1], w[:, GATE_W + 2 * j + 1:GATE_W + 2 * j + 2]]
    q16 = jnp.concatenate(heads, axis=0)
    wcol = jnp.concatenate(wcols, axis=0)
    n_groups = qi // KC + 1

    def idx_body(i, _):
        for u in range(KC):
            kc = i * KC + u
            kk = kidx_ref[pl.ds(pl.multiple_of(kc * QT, QT), QT), :]
            r = jnp.maximum(_nt(q16, kk), 0.0) * wcol
            acc = r[0:QT]
            for h in range(1, IDX_HEADS):
                acc = acc + r[QT * h:QT * (h + 1)]
            key_s[kc] = _order_key(jnp.where(_tile_dist(qi, kc, QT) >= 0, acc, -jnp.inf))
        return 0

    lax.fori_loop(0, n_groups, idx_body, 0)

    def count_ge(cand):
        def body(i, c):
            for u in range(KC):
                c = c + (key_s[i * KC + u] >= cand).astype(jnp.int32)
            return c
        c = lax.fori_loop(0, n_groups, body, jnp.zeros((QT, QT), jnp.int32))
        return jnp.sum(c, axis=-1, keepdims=True)

    thr = _kth_threshold(count_ge, QT, ksel)
    q4 = jnp.concatenate([qc_ref[:, HD * h:HD * (h + 1)] for h in range(DSA_HEADS)], axis=0)

    def att_body(i, carry):
        kcs = [i * KC + u for u in range(KC)]
        ks, vs = [], []
        for kc in kcs:
            kv = kvc_ref[pl.ds(pl.multiple_of(kc * QT, QT), QT), :]
            ks.append(kv[:, 0:HD].astype(BF16))
            vs.append(kv[:, HD:2 * HD].astype(BF16))
        sc = jnp.concatenate([_nt(q4, k) for k in ks], axis=1) * scale
        sc = sc + jnp.concatenate([tb_ref[jnp.clip(qi - kc, 0, 2)] for kc in kcs], axis=1)
        keep = jnp.concatenate([key_s[kc] >= thr for kc in kcs], axis=1)
        dist = jnp.concatenate([_tile_dist(qi, kc, rows) for kc in kcs], axis=1)
        keep = jnp.concatenate([keep] * DSA_HEADS, axis=0) & (dist >= 0)
        return _softmax_step(*carry, sc, keep, _by_block(vs, _mm))

    init = (jnp.full((rows, 1), NEG, F32), jnp.zeros((rows, 1), F32), jnp.zeros((rows, HD), F32))
    _, l, acc = lax.fori_loop(0, n_groups, att_body, init)
    o = _finish(l, acc)
    for h in range(DSA_HEADS):
        o_ref[:, HD * h:HD * (h + 1)] = o[QT * h:QT * (h + 1)]


def _dsa_prompt(qc, qidx, small, kidx2, kvc, tb, nb, s):
    nq = s // QT
    tile = lambda w: pl.BlockSpec((QT, w), lambda b, i: (b * nq + i, 0))
    return pl.pallas_call(
        functools.partial(_dsa_prompt_kernel, ksel=min(IDX_TOPK, s // 4)),
        grid=(nb, nq),
        in_specs=[tile(4 * HD), tile(IDX_HEADS * IDX_DIM), tile(LANE),
                  pl.BlockSpec((None, s, LANE), lambda b, i: (b, 0, 0)),
                  pl.BlockSpec((s, 2 * HD), lambda b, i: (b, 0)),
                  pl.BlockSpec(tb.shape, lambda b, i: (0, 0, 0))],
        out_specs=pl.BlockSpec((QT, 4 * HD), lambda b, i: (b * nq + i, 0)),
        out_shape=jax.ShapeDtypeStruct((nb * s, 4 * HD), F32),
        scratch_shapes=[pltpu.VMEM((nq, QT, QT), jnp.int32)],
        compiler_params=_cparams(("parallel", "arbitrary")),
        name="dsa_prompt",
    )(qc, qidx, small, kidx2, kvc, tb)


def _page_specs(layer, page_shape, n_groups, first_group=0):
    def index(b, g, pt, *, j):
        gg = jnp.clip(g - first_group, 0, n_groups - 1)
        return (layer, pt[b, gg * PG + j], 0, 0)
    return [pl.BlockSpec((None, None) + tuple(page_shape), functools.partial(index, j=j)) for j in range(PG)]


def _seq_spec(shape):
    return pl.BlockSpec((None,) + tuple(shape), lambda b, g, pt: (b,) + (0,) * len(shape))


def _const_spec(a):
    return pl.BlockSpec(a.shape, lambda b, g, pt: (0,) * a.ndim)


def _split_kv(ref):
    n = ref.shape[0] // 2
    return ref[pl.ds(0, n, stride=2), :].astype(BF16), ref[pl.ds(1, n, stride=2), :].astype(BF16)


def _split_kv_pages(pages):
    kvs = [_split_kv(p) for p in pages]
    return jnp.concatenate([k for k, _ in kvs], axis=0), jnp.concatenate([v for _, v in kvs], axis=0)


def _new_key_mask(rows, n_new):
    t = lax.broadcasted_iota(jnp.int32, (rows, PAGE), 0) % TP
    n = lax.broadcasted_iota(jnp.int32, (rows, PAGE), 1)
    return (n <= t) & (n < n_new)


def _s_cmp_kernel(pt_ref, q_ref, *refs, n_pages, n_new, n_pick):
    pages = refs[:PG]
    (new_ref, wh_ref, wt_ref, wn_ref, g_ref, cb_ref, cover_ref,
     ocmp_ref, sel_ref, ck_s, cv_s, carry_s) = refs[PG:]
    g = pl.program_id(1)
    rows = NSA_HEADS * TP
    nj = ck_s.shape[0]
    past = n_pages * PAGE
    row8 = lax.broadcasted_iota(jnp.int32, (8, HD), 0)

    @pl.when(g == 0)
    def _():
        ck_s[...] = jnp.zeros(ck_s.shape, F32)
        cv_s[...] = jnp.zeros(cv_s.shape, F32)
        carry_s[...] = jnp.zeros(carry_s.shape, F32)

    def pair_sums(x):
        y = x[0:8] + x[8:16] + x[16:24] + x[24:32]
        y = y + pltpu.roll(y, 4, 0)
        return y + pltpu.roll(y, 2, 0)

    rows_per_chunk = 2 * CMP_STRIDE
    head = carry_s[...]
    for j in range(PG):
        ck8 = jnp.zeros((8, HD), F32)
        cv8 = jnp.zeros((8, HD), F32)
        for c in range(PAGE // CMP_STRIDE):
            x = pages[j][rows_per_chunk * c:rows_per_chunk * (c + 1), :]
            comp = pair_sums(x * wt_ref[...]) + head
            head = pair_sums(x * wh_ref[...])
            ck8 = jnp.where(row8 == c, comp[0:1], ck8)
            cv8 = jnp.where(row8 == c, comp[1:2], cv8)
        off = pl.multiple_of(g * (PG * 8) + 8 * j, 8)
        ck_s[pl.ds(off, 8), :] = ck8
        cv_s[pl.ds(off, 8), :] = cv8
    carry_s[...] = head
    carry_k = head[0:1]
    carry_v = head[1:2]

    @pl.when(g == pl.num_programs(1) - 1)
    def _():
        new = new_ref[...]
        tail_k = jnp.sum(new[:, 0:HD] * wn_ref[0], axis=0, keepdims=True) + carry_k
        tail_v = jnp.sum(new[:, HD:2 * HD] * wn_ref[1], axis=0, keepdims=True) + carry_v
        last = 8 * n_pages
        ck_s[last:last + 8, :] = jnp.where(row8 == 0, tail_k, 0.0)
        cv_s[last:last + 8, :] = jnp.where(row8 == 0, tail_v, 0.0)

        ck = ck_s[...]
        ms = jnp.mean(ck * ck, axis=-1, keepdims=True)
        ckn = (ck * lax.rsqrt(ms + EPS) * g_ref[...]).astype(BF16)
        s = _nt(q_ref[...], ckn) * HD ** -0.5 + cb_ref[...]
        t = lax.broadcasted_iota(jnp.int32, (rows, nj), 0) % TP
        jj = lax.broadcasted_iota(jnp.int32, (rows, nj), 1)
        mask = (jj >= 1) & (jj <= last) & (past + t - CMP_STRIDE * (jj - 1) - (CMP_LEN - 1) >= 0)
        s = jnp.where(mask, s, NEG)
        e = jnp.where(mask, jnp.exp(s - jnp.max(s, axis=-1, keepdims=True)), 0.0)
        p = e / jnp.maximum(jnp.sum(e, axis=-1, keepdims=True), 1e-30)
        ocmp_ref[...] = _mm(p.astype(BF16), cv_s[...].astype(BF16))
        psum = p[0:TP] + p[TP:2 * TP] + p[2 * TP:3 * TP] + p[3 * TP:4 * TP]
        imp = jnp.dot(psum, cover_ref[...], preferred_element_type=F32, precision=lax.Precision.HIGHEST)
        nsp = imp.shape[1]
        blk = lax.broadcasted_iota(jnp.int32, (TP, nsp), 1)
        qb = (past + lax.broadcasted_iota(jnp.int32, (TP, nsp), 0)) // SEL_BLOCK
        forced = (blk == 0) | (blk == qb) | (blk == qb - 1)
        score = jnp.where(blk <= qb, imp + jnp.where(forced, FORCE_BONUS, 0.0), -jnp.inf)
        sel_ref[...] = _top_blocks(score, n_pick)


def _s_slc_kernel(pt_ref, q_ref, sel_ref, ocmp_ref, small_ref, *refs, n_pages, n_new):
    pages = refs[:PG]
    (nslc_ref, win_ref, nwin_ref, tb_ref, wb_ref, o_ref, m_s, l_s, acc_s) = refs[PG:]
    g = pl.program_id(1)
    rows = NSA_HEADS * TP
    scale = HD ** -0.5
    q = q_ref[...]
    sel = sel_ref[...]
    nsp = sel.shape[1]
    lane_s = lax.broadcasted_iota(jnp.int32, (TP, nsp), 1)
    lane_k = lax.broadcasted_iota(jnp.int32, (TP, PAGE), 1)

    @pl.when(g == 0)
    def _():
        m_s[...] = jnp.full(m_s.shape, NEG, F32)
        l_s[...] = jnp.zeros(l_s.shape, F32)
        acc_s[...] = jnp.zeros(acc_s.shape, F32)

    def block_col(b):
        return jnp.sum(jnp.where(lane_s == b, sel, 0.0), axis=-1, keepdims=True)

    def picked(first_block):
        half = jnp.where(lane_k < SEL_BLOCK, block_col(first_block), block_col(first_block + 1))
        return jnp.concatenate([half] * NSA_HEADS, axis=0) > 0.5

    k_all, v_all = _split_kv_pages(pages)
    last = g == pl.num_programs(1) - 1
    bias = [tb_ref[2]] * (PG - 1) + [tb_ref[jnp.where(last, 0, 2)]]
    sc = _nt(q, k_all) * scale + jnp.concatenate(bias, axis=1)
    keep = jnp.concatenate([picked(2 * (g * PG + j)) for j in range(PG)], axis=1)
    state = _softmax_step(m_s[...], l_s[...], acc_s[...], sc, keep, v_all)
    m_s[...], l_s[...], acc_s[...] = state

    @pl.when(g == pl.num_programs(1) - 1)
    def _():
        new_ok = _new_key_mask(rows, n_new)
        kv = nslc_ref[...]
        sc = _nt(q, kv[:, 0:HD].astype(BF16)) * scale + tb_ref[1]
        _, l, acc = _softmax_step(*state, sc, picked(2 * n_pages) & new_ok, kv[:, HD:2 * HD].astype(BF16))
        o_slc = _finish(l, acc)

        wbuf = win_ref.shape[0] // 2
        k, v = _split_kv(win_ref)
        sc = _nt(q, k) * scale + wb_ref[...]
        t = lax.broadcasted_iota(jnp.int32, (rows, wbuf), 0) % TP
        n = lax.broadcasted_iota(jnp.int32, (rows, wbuf), 1)
        init = (jnp.full((rows, 1), NEG, F32), jnp.zeros((rows, 1), F32), jnp.zeros((rows, HD), F32))
        st = _softmax_step(*init, sc, wbuf + t - n <= WINDOW, v)
        kv = nwin_ref[...]
        sc = _nt(q, kv[:, 0:HD].astype(BF16)) * scale + tb_ref[1]
        _, l, acc = _softmax_step(*st, sc, new_ok, kv[:, HD:2 * HD].astype(BF16))
        o_win = _finish(l, acc)

        gate = small_ref[...]
        o_cmp = ocmp_ref[...]
        for h in range(NSA_HEADS):
            r = slice(TP * h, TP * (h + 1))
            o_ref[:, HD * h:HD * (h + 1)] = (gate[:, 3 * h:3 * h + 1] * o_cmp[r] + gate[:, 3 * h + 1:3 * h + 2] * o_slc[r]
                                             + gate[:, 3 * h + 2:3 * h + 3] * o_win[r])


def _s_mla_kernel(pt_ref, q_ref, *refs, n_new):
    pages = refs[:PG]
    (new_ref, wuv_ref, o_ref, m_s, l_s, acc_s) = refs[PG:]
    g = pl.program_id(1)
    rows = MLA_HEADS * TP
    scale = MLA_QK ** -0.5
    n_qk = KV_LORA + MLA_ROPE
    q = q_ref[:, 0:n_qk]

    @pl.when(g == 0)
    def _():
        m_s[...] = jnp.full(m_s.shape, NEG, F32)
        l_s[...] = jnp.zeros(l_s.shape, F32)
        acc_s[...] = jnp.zeros(acc_s.shape, F32)

    def update(state, page_refs, mask):
        kt = jnp.concatenate([r[0:n_qk, :] for r in page_refs], axis=1).astype(BF16)
        inv = jnp.concatenate([r[n_qk:MLA_ROW, :] for r in page_refs], axis=1)
        mult = jnp.concatenate([jnp.broadcast_to(inv[h:h + 1], (TP, inv.shape[1])) for h in range(MLA_HEADS)], axis=0)
        sc = _mm(q, kt) * (mult * scale)
        return _softmax_step(*state, sc, mask, lambda e: _nt(e, kt[0:KV_LORA]))

    state = update((m_s[...], l_s[...], acc_s[...]), pages, None)
    m_s[...], l_s[...], acc_s[...] = state

    @pl.when(g == pl.num_programs(1) - 1)
    def _():
        _, l, acc = update(state, [new_ref], _new_key_mask(rows, n_new))
        o_lat = _finish(l, acc).astype(BF16)
        for h in range(MLA_HEADS):
            o_ref[:, HD * h:HD * (h + 1)] = _mm(o_lat[TP * h:TP * (h + 1)], wuv_ref[h])


def _s_dsa_kernel(pt_ref, qidx_ref, wb_ref, qc_ref, *refs, n_pages, n_new, ksel):
    kpages = refs[:PG]
    vpages = refs[PG:2 * PG]
    (nkidx_ref, nkv_ref, tb_ref, o_ref, key_s, thr_s, m_s, l_s, acc_s) = refs[2 * PG:]
    g = pl.program_id(1)
    ng = n_pages // PG
    rows = DSA_HEADS * TP
    scale = HD ** -0.5

    def idx_scores(kt):
        r = jnp.maximum(_mm(qidx_ref[...], kt.astype(BF16)), 0.0) * wb_ref[...]
        sc = r[0:TP]
        for h in range(1, IDX_HEADS):
            sc = sc + r[TP * h:TP * (h + 1)]
        return sc

    @pl.when(g == 0)
    def _():
        m_s[...] = jnp.full(m_s.shape, NEG, F32)
        l_s[...] = jnp.zeros(l_s.shape, F32)
        acc_s[...] = jnp.zeros(acc_s.shape, F32)

    @pl.when(g < ng)
    def _():
        keys = _order_key(idx_scores(jnp.concatenate([p[...] for p in kpages], axis=1)))
        for j in range(PG):
            key_s[g * PG + j] = keys[:, PAGE * j:PAGE * (j + 1)]

    @pl.when(g == ng - 1)
    def _():
        sc = jnp.where(_new_key_mask(TP, n_new), idx_scores(nkidx_ref[...]), -jnp.inf)
        key_s[n_pages] = _order_key(sc)

        def count_ge(cand):
            c = jnp.sum((key_s[...] >= cand[None]).astype(jnp.int32), axis=0)
            return jnp.sum(c, axis=-1, keepdims=True)

        thr_s[...] = jnp.broadcast_to(_kth_threshold(count_ge, TP, ksel), thr_s.shape)

    def kept(pidx):
        return jnp.concatenate([key_s[pidx] >= thr_s[...]] * DSA_HEADS, axis=0)

    @pl.when(g >= ng)
    def _():
        k_all, v_all = _split_kv_pages(vpages)
        bias = [tb_ref[2]] * (PG - 1) + [tb_ref[jnp.where(g == 2 * ng - 1, 0, 2)]]
        sc = _nt(qc_ref[...], k_all) * scale + jnp.concatenate(bias, axis=1)
        keep = jnp.concatenate([kept((g - ng) * PG + j) for j in range(PG)], axis=1)
        state = _softmax_step(m_s[...], l_s[...], acc_s[...], sc, keep, v_all)
        m_s[...], l_s[...], acc_s[...] = state

    @pl.when(g == 2 * ng - 1)
    def _():
        kv = nkv_ref[...]
        sc = _nt(qc_ref[...], kv[:, 0:HD].astype(BF16)) * scale + tb_ref[1]
        _, l, acc = _softmax_step(m_s[...], l_s[...], acc_s[...], sc, kept(n_pages) & _new_key_mask(rows, n_new),
                                  kv[:, HD:2 * HD].astype(BF16))
        o = _finish(l, acc)
        for h in range(DSA_HEADS):
            o_ref[:, HD * h:HD * (h + 1)] = o[TP * h:TP * (h + 1)]


def _paged_call(body, page_table, ins, in_specs, out_shapes, out_specs, scratch, n_steps):
    nb = page_table.shape[0]
    return pl.pallas_call(
        body,
        grid_spec=pltpu.PrefetchScalarGridSpec(
            num_scalar_prefetch=1, grid=(nb, n_steps), in_specs=in_specs, out_specs=out_specs,
            scratch_shapes=scratch),
        out_shape=out_shapes,
        compiler_params=_cparams(("parallel", "arbitrary")),
        name=body.func.__name__.strip("_"),
    )(page_table, *ins)


def _softmax_scratch(rows, width):
    return [pltpu.VMEM((rows, 1), F32), pltpu.VMEM((rows, 1), F32), pltpu.VMEM((rows, width), F32)]


def _sample_attention(layer, page_table, n_new, q_a, q_c, q_idx, q_mla, small, w_idx, new, pools, state_win, tabs, wts):
    nb, n_pages = page_table.shape
    ng = n_pages // PG
    past = n_pages * PAGE
    rows4 = NSA_HEADS * TP
    nj = tabs["cb_s"].shape[1]
    nsp = tabs["cover_s"].shape[1]
    kv_page = (2 * PAGE, HD)
    kv_new = (PAGE, 2 * HD)

    ins = ([q_a] + [pools["cmp"]] * PG
           + [new["cmp"], wts["wh"], wts["wt"], wts["wn"], wts["g_cmp"], tabs["cb_s"], tabs["cover_s"]])
    specs = ([_seq_spec((rows4, HD))] + _page_specs(layer, kv_page, ng)
             + [_seq_spec((TP, 2 * HD))] + [_const_spec(a) for a in ins[PG + 2:]])
    o_cmp, sel = _paged_call(
        functools.partial(_s_cmp_kernel, n_pages=n_pages, n_new=n_new, n_pick=min(N_SEL, 2 * n_pages + 1)),
        page_table, ins, specs,
        [jax.ShapeDtypeStruct((nb, rows4, HD), F32), jax.ShapeDtypeStruct((nb, TP, nsp), F32)],
        [_seq_spec((rows4, HD)), _seq_spec((TP, nsp))],
        [pltpu.VMEM((nj, HD), F32), pltpu.VMEM((nj, HD), F32), pltpu.VMEM((8, HD), F32)], ng)

    ins = ([q_a, sel, o_cmp, small] + [pools["slc"]] * PG
           + [new["slc"], state_win, new["win"], tabs["tbs_a"], tabs["wb"]])
    specs = ([_seq_spec((rows4, HD)), _seq_spec((TP, nsp)), _seq_spec((rows4, HD)), _seq_spec((TP, LANE))]
             + _page_specs(layer, kv_page, ng)
             + [_seq_spec(kv_new),
                pl.BlockSpec((None, None) + state_win.shape[2:], lambda b, g, pt: (layer, b, 0, 0)),
                _seq_spec(kv_new), _const_spec(tabs["tbs_a"]), _const_spec(tabs["wb"])])
    o_a = _paged_call(
        functools.partial(_s_slc_kernel, n_pages=n_pages, n_new=n_new),
        page_table, ins, specs, jax.ShapeDtypeStruct((nb, TP, NSA_HEADS * HD), F32),
        _seq_spec((TP, NSA_HEADS * HD)), _softmax_scratch(rows4, HD), ng)

    rows8 = MLA_HEADS * TP
    ins = [q_mla] + [pools["mla"]] * PG + [new["mla"], wts["wuv"]]
    specs = ([_seq_spec((rows8, ROWP))] + _page_specs(layer, (MLA_ROW, PAGE), ng)
             + [_seq_spec((MLA_ROW, PAGE)), _const_spec(wts["wuv"])])
    o_b = _paged_call(
        functools.partial(_s_mla_kernel, n_new=n_new),
        page_table, ins, specs, jax.ShapeDtypeStruct((nb, TP, MLA_HEADS * HD), F32),
        _seq_spec((TP, MLA_HEADS * HD)), _softmax_scratch(rows8, KV_LORA), ng)

    rows16 = IDX_HEADS * TP
    ins = ([q_idx, w_idx, q_c] + [pools["kidx"]] * PG + [pools["dkv"]] * PG
           + [new["kidx"], new["dkv"], tabs["tbs_c"]])
    specs = ([_seq_spec((rows16, IDX_DIM)), _seq_spec((rows16, 1)), _seq_spec((rows4, HD))]
             + _page_specs(layer, (IDX_DIM, PAGE), ng) + _page_specs(layer, kv_page, ng, first_group=ng)
             + [_seq_spec((IDX_DIM, PAGE)), _seq_spec(kv_new), _const_spec(tabs["tbs_c"])])
    o_c = _paged_call(
        functools.partial(_s_dsa_kernel, n_pages=n_pages, n_new=n_new, ksel=min(IDX_TOPK, (past + n_new) // 4)),
        page_table, ins, specs, jax.ShapeDtypeStruct((nb, TP, DSA_HEADS * HD), F32),
        _seq_spec((TP, DSA_HEADS * HD)),
        [pltpu.VMEM((n_pages + 1, TP, PAGE), jnp.int32), pltpu.VMEM((TP, PAGE), jnp.int32)]
        + _softmax_scratch(rows4, HD), 2 * ng)
    return o_a, o_b, o_c


def _t5_bucket(n):
    max_exact = NUM_BUCKETS // 2
    nf = jnp.maximum(n, 1).astype(F32)
    large = max_exact + (jnp.log(nf / max_exact) / math.log(MAX_DISTANCE / max_exact)
                         * (NUM_BUCKETS - max_exact)).astype(jnp.int32)
    return jnp.where(n < max_exact, n, jnp.minimum(large, NUM_BUCKETS - 1))


def _round_up(x, m):
    return -(-x // m) * m


def _cover(n_rows, n_cols, first_row, n_valid, n_blocks):
    n = np.arange(n_rows)[:, None] - first_row
    s = np.arange(n_cols)[None]
    shared = np.maximum(np.minimum(CMP_STRIDE * n + CMP_LEN, SEL_BLOCK * s + SEL_BLOCK)
                        - np.maximum(CMP_STRIDE * n, SEL_BLOCK * s), 0)
    ok = (n >= 0) & (n < n_valid) & (s < n_blocks)
    return jnp.asarray(np.where(ok, shared / CMP_LEN, 0.0), F32)


def _tables(rel_bias, s, n_pages, n_new, wbuf):
    nq = s // QT
    past = n_pages * PAGE
    def look(heads, dist):
        bucket = _t5_bucket(dist)[None]
        rb = rel_bias[:, heads].astype(F32)
        out = jnp.zeros((rb.shape[1],) + dist.shape, F32)
        for k in range(NUM_BUCKETS):
            out = out + jnp.where(bucket == k, rb[k].reshape((-1,) + (1,) * dist.ndim), 0.0)
        return out

    nsa, dsa = slice(0, NSA_HEADS), slice(NSA_HEADS, NSA_HEADS + DSA_HEADS)
    t = jnp.arange(QT)[:, None]
    c = jnp.arange(QT)[None]
    tb = lambda heads: jnp.stack([look(heads, QT * d + t - c) for d in range(3)]).reshape(3, -1, QT)
    ncp = s // CMP_STRIDE
    cb_p = look(nsa, jnp.arange(s)[:, None] - CMP_STRIDE * jnp.arange(ncp)[None] - (CMP_LEN - 1))
    cb_p = cb_p.reshape(NSA_HEADS, nq, QT, ncp).transpose(1, 0, 2, 3).reshape(nq, NSA_HEADS * QT, ncp)

    t8 = jnp.arange(TP)[:, None]
    n = jnp.arange(PAGE)[None]
    tbs = lambda heads: jnp.stack([look(heads, PAGE + t8 - n), look(heads, t8 - n),
                                   look(heads, jnp.full((TP, PAGE), FAR))]).reshape(3, -1, PAGE)
    nj = _round_up(8 * n_pages + 8, LANE)
    n_sel_blocks = -(-(past + n_new) // SEL_BLOCK)
    etab = (np.arange(s // SEL_BLOCK)[None, :, None]
            == (np.arange(nq)[:, None, None] * QT + np.arange(QT)[None, None, :]) // SEL_BLOCK)
    return dict(
        tb_a=tb(nsa), tb_c=tb(dsa), cb_p=cb_p,
        cover_p=_cover(ncp, s // SEL_BLOCK, 0, ncp - 1, s // SEL_BLOCK),
        etab=jnp.asarray(etab, BF16),
        tbs_a=tbs(nsa), tbs_c=tbs(dsa),
        wb=look(nsa, wbuf + t8 - jnp.arange(wbuf)[None]).reshape(-1, wbuf),
        cb_s=look(nsa, past + t8 - CMP_STRIDE * (jnp.arange(nj)[None] - 1) - (CMP_LEN - 1)).reshape(-1, nj),
        cover_s=_cover(nj, _round_up(n_sel_blocks, LANE), 1, 8 * n_pages, n_sel_blocks),
    )


_PROJ_SRC = np.concatenate([
    np.arange(0, 512), np.arange(512, 1280),
    np.arange(2124, 2636), np.arange(2636, 2892),
    np.arange(2892, 3916),
    np.arange(1292, 1804), np.arange(1804, 2060),
    np.arange(2060, 2124), np.arange(3916, 3980),
    np.arange(1280, 1292), np.arange(3980, 3996), np.full(LANE - GATE_W - IDX_HEADS, -1),
])


def _layer_weights(l, w_in, w_nsa_cmp, g_nsa_qk, g_mla_qlat, g_mla_kvlat, w_mla_uq, w_mla_uk, w_mla_uv,
                   g_mla_q, g_mla_k, g_dsa_qk, s):
    src = jnp.asarray(np.maximum(_PROJ_SRC, 0))
    w_in_p = jnp.where(jnp.asarray(_PROJ_SRC >= 0)[None], jnp.take(w_in[l], src, axis=1), 0.0).astype(BF16)

    def row(v):
        return jnp.pad(v.astype(F32), (0, Q_LORA - v.shape[0]))
    zeros = jnp.zeros((LANE - MLA_ROPE,), F32)
    gpack = jnp.stack([
        row(g_nsa_qk[l, 0]), row(g_nsa_qk[l, 2]), row(g_nsa_qk[l, 3]), row(g_dsa_qk[l, 0]), row(g_dsa_qk[l, 1]),
        row(g_mla_qlat[l]), row(g_mla_kvlat[l]), row(jnp.concatenate([g_mla_q[l], zeros])),
        row(g_mla_k[l, :MLA_NOPE]), row(jnp.concatenate([g_mla_k[l, MLA_NOPE:], zeros]))]
        + [jnp.zeros((Q_LORA,), F32)] * 6)

    wc = w_nsa_cmp[l].astype(F32)
    ncp = s // CMP_STRIDE
    off = np.arange(s)[None] - CMP_STRIDE * np.arange(ncp)[:, None]
    ok = (off >= 0) & (off < CMP_LEN) & (np.arange(ncp)[:, None] < ncp - 1)
    band = lambda k: jnp.where(jnp.asarray(ok), wc[k][jnp.asarray(np.clip(off, 0, CMP_LEN - 1))], 0.0).astype(BF16)
    rows_w = lambda w: jnp.broadcast_to(w.T.reshape(2 * CMP_STRIDE, 1), (2 * CMP_STRIDE, HD))
    wh, wt = rows_w(wc[:, :CMP_STRIDE]), rows_w(wc[:, CMP_STRIDE:])
    wn = jnp.broadcast_to(wc[:, CMP_STRIDE:CMP_STRIDE + TP, None], (2, TP, HD))

    return dict(
        w_in=w_in_p, gpack=gpack,
        wuq2=jnp.pad(w_mla_uq[l], ((0, 0), (0, 0), (0, LANE - MLA_ROPE))).reshape(Q_LORA, -1).astype(BF16),
        wukt=w_mla_uk[l].transpose(1, 2, 0).astype(BF16),
        wukf=w_mla_uk[l].reshape(KV_LORA, -1).astype(BF16),
        wuv=w_mla_uv[l].transpose(1, 0, 2).astype(BF16),
        band_k=band(0), band_v=band(1), wh=wh, wt=wt, wn=wn,
        g_cmp=g_nsa_qk[l, 1].reshape(1, HD).astype(F32),
    )


def kernel(x_prompt, x_sample, cache_nsa_cmp_kv, cache_nsa_slc_kv, state_nsa_win_kv, cache_mla, cache_dsa_kv, cache_dsa_kidx, page_table, rel_bias, g_mix, w_in, g_nsa_qk, w_nsa_cmp, g_mla_qlat, g_mla_kvlat, w_mla_uq, w_mla_uk, w_mla_uv, g_mla_q, g_mla_k, g_dsa_qk, w_out, g_ffn, w_up, w_down):
    nb, s, _ = x_prompt.shape
    nd, n_new, _ = x_sample.shape
    depth, n_pool = cache_mla.shape[:2]
    n_pages = page_table.shape[1]
    past = n_pages * PAGE
    wbuf = state_nsa_win_kv.shape[2]
    assert s % (KC * QT) == 0 and n_pages % PG == 0 and n_new <= TP and wbuf >= WINDOW
    mp, ms = nb * s, nd * n_new
    m_pad = _round_up(mp + ms, MM_T)
    pad_rows = lambda a: jnp.pad(a, ((0, m_pad - a.shape[0]), (0, 0)))
    smp = slice(mp, mp + ms)

    x = pad_rows(jnp.concatenate([x_prompt.reshape(mp, D_MODEL), x_sample.reshape(ms, D_MODEL)]))
    pos = jnp.concatenate([jnp.tile(jnp.arange(s), nb), jnp.tile(past + jnp.arange(n_new), nd)]).astype(F32)
    half = MLA_ROPE // 2
    ang = pos[:, None] * (ROPE_THETA ** (-jnp.arange(half, dtype=F32) / half))[None]
    cos, sin, z = jnp.cos(ang), jnp.sin(ang), jnp.zeros((mp + ms, LANE - MLA_ROPE), F32)
    cs = pad_rows(jnp.concatenate([cos, cos, z, -sin, sin, z], axis=1))

    tabs = _tables(rel_bias, s, n_pages, n_new, wbuf)
    pools = dict(
        cmp=cache_nsa_cmp_kv.reshape(depth, n_pool, 2 * PAGE, HD),
        slc=cache_nsa_slc_kv.reshape(depth, n_pool, 2 * PAGE, HD),
        dkv=cache_dsa_kv.reshape(depth, n_pool, 2 * PAGE, HD),
        kidx=jnp.swapaxes(cache_dsa_kidx, -1, -2), mla=jnp.swapaxes(cache_mla, -1, -2))
    state_win = state_nsa_win_kv.reshape(depth, nd, 2 * wbuf, HD)

    def per_seq(a, rows):
        a = a.reshape(nd, n_new, a.shape[-1])
        return jnp.pad(a, ((0, 0), (0, rows - n_new), (0, 0)))

    def heads_first(a, n_heads):
        a = per_seq(a, TP).reshape(nd, TP, n_heads, -1)
        return a.transpose(0, 2, 1, 3).reshape(nd, n_heads * TP, -1)

    prompt_new, sample_new = [], []
    for l in range(depth):
        wts = _layer_weights(l, w_in, w_nsa_cmp, g_nsa_qk, g_mla_qlat, g_mla_kvlat, w_mla_uq, w_mla_uk,
                             w_mla_uv, g_mla_q, g_mla_k, g_dsa_qk, s)
        proj = _mm_full(x, wts["w_in"], gain=g_mix[l])
        qa, slc, win, kvc, qc, qidx, small, qmla, rowp = _post(
            proj, cs, wts["gpack"], wts["wuq2"], wts["wukt"], wts["wukf"])
        kvcmp = proj[:, C_CMP:C_CMP + 2 * HD]
        kidx = proj[:, C_A + MLA_ROPE:C_A + LANE]
        inv_r = rowp[:, KV_LORA + MLA_ROPE:MLA_ROW]

        ck, cv = _compress_prompt(kvcmp, wts["band_k"], wts["band_v"], wts["g_cmp"], nb, s)
        o_a = _nsa_prompt(qa, small, ck, cv, slc, win, tabs["tb_a"], tabs["cb_p"], tabs["cover_p"],
                          tabs["etab"], nb, s)
        inv_p = jnp.swapaxes(inv_r[:mp].reshape(nb, s // QT, QT, MLA_HEADS), -1, -2)
        o_b = _mla_prompt(qmla, rowp, inv_p, wts["wuv"], nb, s)
        kidx2 = jnp.concatenate([kidx[:mp], kidx[:mp]], axis=1).reshape(nb, s, LANE).astype(BF16)
        o_c = _dsa_prompt(qc, qidx, small, kidx2, kvc, tabs["tb_c"], nb, s)

        small_s = per_seq(small[smp], TP)
        w_idx = small_s[:, :, GATE_W:GATE_W + IDX_HEADS].transpose(0, 2, 1).reshape(nd, IDX_HEADS * TP, 1)
        new = dict(cmp=per_seq(kvcmp[smp], TP), slc=per_seq(slc[smp], PAGE), win=per_seq(win[smp], PAGE),
                   mla=jnp.swapaxes(per_seq(rowp[smp, :MLA_ROW], PAGE), -1, -2),
                   kidx=jnp.swapaxes(per_seq(kidx[smp], PAGE), -1, -2), dkv=per_seq(kvc[smp], PAGE))
        so_a, so_b, so_c = _sample_attention(
            l, page_table, n_new, heads_first(qa[smp], NSA_HEADS), heads_first(qc[smp], DSA_HEADS),
            heads_first(qidx[smp], IDX_HEADS), heads_first(qmla[smp], MLA_HEADS), small_s,
            w_idx, new, pools, state_win, tabs, wts)

        mixed = pad_rows(jnp.concatenate([
            jnp.concatenate([o_a, o_b, o_c], axis=1),
            jnp.concatenate([so_a[:, :n_new], so_b[:, :n_new], so_c[:, :n_new]], axis=2).reshape(ms, -1)]))
        x1 = _mm_full(mixed, w_out[l].astype(BF16), res=x)
        hidden = _mm_full(x1, w_up[l].astype(BF16), gain=g_ffn[l], act="relu2", out_dtype=BF16)
        x = _mm_ktiled(hidden, w_down[l].astype(BF16), x1)

        kv4 = lambda a, r, n: a[r].reshape(n, -1, 2, HD)
        pr = slice(0, mp)
        prompt_new.append((kv4(kvcmp, pr, nb), kv4(slc, pr, nb), kv4(win, pr, nb)[:, -min(WINDOW, s):],
                           rowp[pr, :MLA_ROW].reshape(nb, s, MLA_ROW), kv4(kvc, pr, nb),
                           kidx[pr].reshape(nb, s, IDX_DIM)))
        win_all = jnp.concatenate([state_nsa_win_kv[l], kv4(win, smp, nd)], axis=1)
        sample_new.append((kv4(kvcmp, smp, nd), kv4(slc, smp, nd), win_all[:, -min(WINDOW, past + n_new):],
                           rowp[smp, :MLA_ROW].reshape(nd, n_new, MLA_ROW), kv4(kvc, smp, nd),
                           kidx[smp].reshape(nd, n_new, IDX_DIM)))

    outs_p = [jnp.stack(a) for a in zip(*prompt_new)]
    outs_s = [jnp.stack(a) for a in zip(*sample_new)]
    return (x[:mp].reshape(nb, s, D_MODEL), x[smp].reshape(nd, n_new, D_MODEL), *outs_p, *outs_s)
```

```python
import functools
import math

import numpy as np
import jax
import jax.numpy as jnp
from jax import lax
from jax.experimental import pallas as pl
from jax.experimental.pallas import tpu as pltpu

F32 = jnp.float32
BF16 = jnp.bfloat16

D_MODEL = 2048
PAGE = 128
NSA_HEADS = 4
HD = 128
CMP_LEN = 32
CMP_STRIDE = 16
SEL_BLOCK = 64
N_SEL = 16
WINDOW = 512
MLA_HEADS = 8
MLA_NOPE = 128
MLA_ROPE = 64
MLA_QK = MLA_NOPE + MLA_ROPE
Q_LORA = 512
KV_LORA = 256
MLA_ROW = KV_LORA + MLA_ROPE + MLA_HEADS
DSA_HEADS = 4
IDX_HEADS = 16
IDX_DIM = 64
IDX_TOPK = 256
NUM_BUCKETS = 32
MAX_DISTANCE = 128
ROPE_THETA = 10000.0
EPS = 1e-6
NEG = -1e30
FORCE_BONUS = 1e4

LANE = 128
QT = 128
KC = 4
TP = 8
PG = 64
ROWP = 384
MM_T = 512
MM_TK = 2048
POST_T = 256
VMEM_LIMIT = 56 * 1024 * 1024
FAR = 2 * MAX_DISTANCE

C_QA, C_CMP, C_SLC, C_WIN, C_QC, C_KVC, C_QIDX, C_DQ, C_DKV, C_A, C_B, PROJ_P = (
    0, 512, 768, 1024, 1280, 1792, 2048, 3072, 3584, 3840, 3968, 4096)
GATE_W = 3 * NSA_HEADS


def _nt(a, b):
    return lax.dot_general(a, b, (((1,), (1,)), ((), ())), preferred_element_type=F32)


def _mm(a, b):
    return jnp.dot(a, b, preferred_element_type=F32)


def _cparams(sem):
    return pltpu.CompilerParams(dimension_semantics=sem, vmem_limit_bytes=VMEM_LIMIT)


def _softmax_step(m, l, acc, s, mask, v):
    if mask is not None:
        s = jnp.where(mask, s, NEG)
    m_new = jnp.maximum(m, jnp.max(s, axis=-1, keepdims=True))
    alpha = jnp.exp(m - m_new)
    e = jnp.exp(s - m_new)
    if mask is not None:
        e = jnp.where(mask, e, 0.0)
    l = alpha * l + jnp.sum(e, axis=-1, keepdims=True)
    e = e.astype(BF16)
    acc = alpha * acc + (v(e) if callable(v) else _mm(e, v))
    return m_new, l, acc


def _by_block(blocks, fn):
    def pv(e):
        out = fn(e[:, 0:QT], blocks[0])
        for j in range(1, len(blocks)):
            out = out + fn(e[:, QT * j:QT * (j + 1)], blocks[j])
        return out
    return pv


def _finish(l, acc):
    return acc / jnp.maximum(l, 1e-30)


def _top_blocks(score, n_pick):
    lane = lax.broadcasted_iota(jnp.int32, score.shape, 1)
    width = score.shape[1]
    sel = jnp.zeros(score.shape, F32)
    for _ in range(n_pick):
        mx = jnp.max(score, axis=-1, keepdims=True)
        idx = jnp.min(jnp.where(score == mx, lane, width), axis=-1, keepdims=True)
        hit = lane == idx
        sel = jnp.where(hit, 1.0, sel)
        score = jnp.where(hit, -jnp.inf, score)
    return sel


def _order_key(x):
    bits = lax.bitcast_convert_type(x, jnp.int32)
    return jnp.where(bits < 0, bits ^ jnp.int32(0x7FFFFFFF), bits)


def _kth_threshold(count_ge, rows, k):
    def step(i, t):
        cand = t + lax.shift_left(jnp.int32(1), jnp.int32(31) - i)
        return jnp.where(count_ge(cand) >= k, cand, t)
    t0 = jnp.full((rows, 1), jnp.iinfo(jnp.int32).min, jnp.int32)
    return lax.fori_loop(0, 32, step, t0)


def _mm_full_kernel(*refs, has_gain, has_res, act):
    a_ref, w_ref = refs[0], refs[1]
    k = 2
    g_ref = r_ref = None
    if has_gain:
        g_ref = refs[k]; k += 1
    if has_res:
        r_ref = refs[k]; k += 1
    o_ref, abuf = refs[k], refs[k + 1]

    @pl.when(pl.program_id(1) == 0)
    def _():
        a = a_ref[...].astype(F32)
        if has_gain:
            ms = jnp.mean(a * a, axis=-1, keepdims=True)
            a = a * lax.rsqrt(ms + EPS) * g_ref[...]
        abuf[...] = a.astype(BF16)

    y = _mm(abuf[...], w_ref[...])
    if act == "relu2":
        y = jnp.square(jnp.maximum(y, 0.0))
    if has_res:
        y = y + r_ref[...]
    o_ref[...] = y.astype(o_ref.dtype)


def _mm_full(a, w, gain=None, res=None, act=None, out_dtype=F32):
    m, kdim = a.shape
    n = w.shape[1]
    ins = [a, w]
    specs = [pl.BlockSpec((MM_T, kdim), lambda i, j: (i, 0)),
             pl.BlockSpec((kdim, MM_T), lambda i, j: (0, j))]
    if gain is not None:
        ins.append(gain.reshape(1, kdim).astype(F32))
        specs.append(pl.BlockSpec((1, kdim), lambda i, j: (0, 0)))
    if res is not None:
        ins.append(res)
        specs.append(pl.BlockSpec((MM_T, MM_T), lambda i, j: (i, j)))
    return pl.pallas_call(
        functools.partial(_mm_full_kernel, has_gain=gain is not None, has_res=res is not None, act=act),
        grid=(m // MM_T, n // MM_T),
        in_specs=specs,
        out_specs=pl.BlockSpec((MM_T, MM_T), lambda i, j: (i, j)),
        out_shape=jax.ShapeDtypeStruct((m, n), out_dtype),
        scratch_shapes=[pltpu.VMEM((MM_T, kdim), BF16)],
        compiler_params=_cparams(("parallel", "arbitrary")),
        name="matmul_" + (act or ("norm" if gain is not None else "plain")),
    )(*ins)


def _mm_kt_kernel(a_ref, w_ref, r_ref, o_ref, acc):
    kk = pl.program_id(2)

    @pl.when(kk == 0)
    def _():
        acc[...] = r_ref[...]

    acc[...] += _mm(a_ref[...], w_ref[...])

    @pl.when(kk == pl.num_programs(2) - 1)
    def _():
        o_ref[...] = acc[...]


def _mm_ktiled(a, w, res):
    m, kdim = a.shape
    n = w.shape[1]
    return pl.pallas_call(
        _mm_kt_kernel,
        grid=(m // MM_T, n // MM_T, kdim // MM_TK),
        in_specs=[pl.BlockSpec((MM_T, MM_TK), lambda i, j, k: (i, k)),
                  pl.BlockSpec((MM_TK, MM_T), lambda i, j, k: (k, j)),
                  pl.BlockSpec((MM_T, MM_T), lambda i, j, k: (i, j))],
        out_specs=pl.BlockSpec((MM_T, MM_T), lambda i, j, k: (i, j)),
        out_shape=jax.ShapeDtypeStruct((m, n), F32),
        scratch_shapes=[pltpu.VMEM((MM_T, MM_T), F32)],
        compiler_params=_cparams(("parallel", "parallel", "arbitrary")),
        name="matmul_ktiled",
    )(a, w, res)


def _post_kernel(p_ref, cs_ref, g_ref, wuq_ref, wukt_ref, wukf_ref,
                 qa_ref, slc_ref, win_ref, kvc_ref, qc_ref, qidx_ref, small_ref, qmla_ref, row_ref):
    def nrm(x, g):
        ms = jnp.mean(x * x, axis=-1, keepdims=True)
        return x * lax.rsqrt(ms + EPS) * g

    def gain(r, w):
        return g_ref[r:r + 1, 0:w]

    for h in range(NSA_HEADS):
        qa_ref[:, HD * h:HD * (h + 1)] = nrm(p_ref[:, C_QA + HD * h:C_QA + HD * (h + 1)], gain(0, HD)).astype(BF16)
        qc_ref[:, HD * h:HD * (h + 1)] = nrm(p_ref[:, C_QC + HD * h:C_QC + HD * (h + 1)], gain(3, HD)).astype(BF16)
    for ref, col, r in ((slc_ref, C_SLC, 1), (win_ref, C_WIN, 2), (kvc_ref, C_KVC, 4)):
        ref[:, 0:HD] = nrm(p_ref[:, col:col + HD], gain(r, HD))
        ref[:, HD:2 * HD] = p_ref[:, col + HD:col + 2 * HD]
    qidx_ref[...] = (p_ref[:, C_QIDX:C_QIDX + IDX_HEADS * IDX_DIM] * IDX_DIM ** -0.5).astype(BF16)

    tm = p_ref.shape[0]
    lane = lax.broadcasted_iota(jnp.int32, (tm, LANE), 1)
    blk_b = p_ref[:, C_B:C_B + LANE]
    small_ref[...] = jnp.where(lane < GATE_W, jax.nn.sigmoid(blk_b), blk_b * IDX_HEADS ** -0.5)

    cos = cs_ref[:, 0:LANE]
    sin = cs_ref[:, LANE:2 * LANE]

    def rope(x):
        swapped = jnp.where(lane < MLA_ROPE // 2, pltpu.roll(x, LANE - MLA_ROPE // 2, 1),
                            pltpu.roll(x, MLA_ROPE // 2, 1))
        return x * cos + swapped * sin

    cq = nrm(p_ref[:, C_DQ:C_DQ + Q_LORA], gain(5, Q_LORA)).astype(BF16)
    q2 = _mm(cq, wuq_ref[...])
    for h in range(MLA_HEADS):
        blk = q2[:, 2 * LANE * h:2 * LANE * (h + 1)]
        ms = jnp.sum(blk * blk, axis=-1, keepdims=True) * (1.0 / MLA_QK)
        qn = blk * lax.rsqrt(ms + EPS) * gain(7, 2 * LANE)
        qlat = _mm((qn[:, 0:MLA_NOPE] * gain(8, MLA_NOPE)).astype(BF16), wukt_ref[h])
        qmla_ref[:, ROWP * h:ROWP * h + KV_LORA] = qlat.astype(BF16)
        qmla_ref[:, ROWP * h + KV_LORA:ROWP * (h + 1)] = rope(qn[:, MLA_NOPE:2 * LANE]).astype(BF16)

    ckv = nrm(p_ref[:, C_DKV:C_DKV + KV_LORA], gain(6, KV_LORA))
    kn = _mm(ckv.astype(BF16), wukf_ref[...])
    kpe = jnp.where(lane < MLA_ROPE, p_ref[:, C_A:C_A + LANE], 0.0)
    ssq_pe = jnp.sum(kpe * kpe, axis=-1, keepdims=True)
    third = rope(kpe * gain(9, LANE))
    for h in range(MLA_HEADS):
        kh = kn[:, MLA_NOPE * h:MLA_NOPE * (h + 1)]
        ms = (jnp.sum(kh * kh, axis=-1, keepdims=True) + ssq_pe) * (1.0 / MLA_QK)
        third = jnp.where(lane == MLA_ROPE + h, lax.rsqrt(ms + EPS), third)
    row_ref[:, 0:KV_LORA] = ckv
    row_ref[:, KV_LORA:ROWP] = third


def _post(proj, cs, gpack, wuq2, wukt, wukf):
    m = proj.shape[0]
    row = lambda w: pl.BlockSpec((POST_T, w), lambda i: (i, 0))
    full = lambda a: pl.BlockSpec(a.shape, lambda i: (0,) * a.ndim)
    widths = (4 * HD, 2 * HD, 2 * HD, 2 * HD, 4 * HD, IDX_HEADS * IDX_DIM, LANE, MLA_HEADS * ROWP, ROWP)
    dtypes = (BF16, F32, F32, F32, BF16, BF16, F32, BF16, F32)
    return pl.pallas_call(
        _post_kernel,
        grid=(m // POST_T,),
        in_specs=[row(PROJ_P), row(2 * LANE), full(gpack), full(wuq2), full(wukt), full(wukf)],
        out_specs=[row(w) for w in widths],
        out_shape=[jax.ShapeDtypeStruct((m, w), d) for w, d in zip(widths, dtypes)],
        compiler_params=_cparams(("parallel",)),
        name="post_projection",
    )(proj, cs, gpack, wuq2, wukt, wukf)


def _compress_kernel(kv_ref, wk_ref, wv_ref, g_ref, ck_ref, cv_ref):
    ck = _mm(wk_ref[...], kv_ref[:, 0:HD].astype(BF16))
    ms = jnp.mean(ck * ck, axis=-1, keepdims=True)
    ck_ref[...] = (ck * lax.rsqrt(ms + EPS) * g_ref[...]).astype(BF16)
    cv_ref[...] = _mm(wv_ref[...], kv_ref[:, HD:2 * HD].astype(BF16)).astype(BF16)


def _compress_prompt(kvcmp, wk, wv, g, nb, s):
    ncp = wk.shape[0]
    return pl.pallas_call(
        _compress_kernel,
        grid=(nb,),
        in_specs=[pl.BlockSpec((s, 2 * HD), lambda b: (b, 0)),
                  pl.BlockSpec(wk.shape, lambda b: (0, 0)),
                  pl.BlockSpec(wv.shape, lambda b: (0, 0)),
                  pl.BlockSpec((1, HD), lambda b: (0, 0))],
        out_specs=[pl.BlockSpec((None, ncp, HD), lambda b: (b, 0, 0))] * 2,
        out_shape=[jax.ShapeDtypeStruct((nb, ncp, HD), BF16)] * 2,
        compiler_params=_cparams(("parallel",)),
        name="compress_prompt",
    )(kvcmp, wk, wv, g)


def _tile_dist(qi, kc, rows):
    t = lax.broadcasted_iota(jnp.int32, (rows, QT), 0) % QT
    c = lax.broadcasted_iota(jnp.int32, (rows, QT), 1)
    return QT * (qi - kc) + t - c


def _nsa_prompt_kernel(qa_ref, small_ref, ck_ref, cv_ref, slc_ref, win_ref, tb_ref, cb_ref, cover_ref,
                       e_ref, o_ref, *, n_cmp, n_pick):
    qi = pl.program_id(1)
    rows = NSA_HEADS * QT
    scale = HD ** -0.5
    q4 = jnp.concatenate([qa_ref[:, HD * h:HD * (h + 1)] for h in range(NSA_HEADS)], axis=0)

    ncp = ck_ref.shape[0]
    s = _nt(q4, ck_ref[...]) * scale + cb_ref[...]
    t = lax.broadcasted_iota(jnp.int32, (rows, ncp), 0) % QT
    ci = lax.broadcasted_iota(jnp.int32, (rows, ncp), 1)
    mask = (QT * qi + t - CMP_STRIDE * ci - (CMP_LEN - 1) >= 0) & (ci < n_cmp)
    s = jnp.where(mask, s, NEG)
    e = jnp.where(mask, jnp.exp(s - jnp.max(s, axis=-1, keepdims=True)), 0.0)
    p = e / jnp.maximum(jnp.sum(e, axis=-1, keepdims=True), 1e-30)
    o_cmp = _mm(p.astype(BF16), cv_ref[...])
    psum = p[0:QT] + p[QT:2 * QT] + p[2 * QT:3 * QT] + p[3 * QT:4 * QT]
    imp = jnp.dot(psum, cover_ref[...], preferred_element_type=F32, precision=lax.Precision.HIGHEST)
    ns = imp.shape[1]
    blk = lax.broadcasted_iota(jnp.int32, (QT, ns), 1)
    qb = (QT * qi + lax.broadcasted_iota(jnp.int32, (QT, ns), 0)) // SEL_BLOCK
    forced = (blk == 0) | (blk == qb) | (blk == qb - 1)
    score = jnp.where(blk <= qb, imp + jnp.where(forced, FORCE_BONUS, 0.0), -jnp.inf)
    sel = _top_blocks(score, n_pick).astype(BF16)

    init = (jnp.full((rows, 1), NEG, F32), jnp.zeros((rows, 1), F32), jnp.zeros((rows, HD), F32))

    def logits(ref, kcs):
        ks, vs = [], []
        for kc in kcs:
            kv = ref[pl.ds(pl.multiple_of(kc * QT, QT), QT), :]
            ks.append(kv[:, 0:HD].astype(BF16))
            vs.append(kv[:, HD:2 * HD].astype(BF16))
        sc = jnp.concatenate([_nt(q4, k) for k in ks], axis=1) * scale
        bias = jnp.concatenate([tb_ref[jnp.clip(qi - kc, 0, 2)] for kc in kcs], axis=1)
        return sc + bias, vs

    def slc_body(i, carry):
        kcs = [i * KC + u for u in range(KC)]
        sc, vs = logits(slc_ref, kcs)
        picked = jnp.concatenate([_mm(sel, e_ref[kc]) for kc in kcs], axis=1)
        picked = jnp.concatenate([picked] * NSA_HEADS, axis=0)
        dist = jnp.concatenate([_tile_dist(qi, kc, rows) for kc in kcs], axis=1)
        return _softmax_step(*carry, sc, (picked > 0.5) & (dist >= 0), _by_block(vs, _mm))

    _, l, acc = lax.fori_loop(0, qi // KC + 1, slc_body, init)
    o_slc = _finish(l, acc)

    first = qi - WINDOW // QT
    kcs = [first + u for u in range(WINDOW // QT + 1)]
    sc, vs = logits(win_ref, [jnp.maximum(kc, 0) for kc in kcs])
    ok = []
    for kc in kcs:
        dist = _tile_dist(qi, kc, rows)
        ok.append((dist >= 0) & (dist <= WINDOW) & (kc >= 0))
    _, l, acc = _softmax_step(*init, sc, jnp.concatenate(ok, axis=1), _by_block(vs, _mm))
    o_win = _finish(l, acc)

    gate = small_ref[...]
    for h in range(NSA_HEADS):
        r = slice(QT * h, QT * (h + 1))
        o_ref[:, HD * h:HD * (h + 1)] = (gate[:, 3 * h:3 * h + 1] * o_cmp[r] + gate[:, 3 * h + 1:3 * h + 2] * o_slc[r]
                                         + gate[:, 3 * h + 2:3 * h + 3] * o_win[r])


def _nsa_prompt(qa, small, ck, cv, slc, win, tb, cb, cover, etab, nb, s):
    nq = s // QT
    ncp = ck.shape[1]
    rows = NSA_HEADS * QT
    tile = lambda w: pl.BlockSpec((QT, w), lambda b, i: (b * nq + i, 0))
    seq = lambda w: pl.BlockSpec((s, w), lambda b, i: (b, 0))
    const = lambda a: pl.BlockSpec(a.shape, lambda b, i: (0,) * a.ndim)
    return pl.pallas_call(
        functools.partial(_nsa_prompt_kernel, n_cmp=s // CMP_STRIDE - 1, n_pick=min(N_SEL, s // SEL_BLOCK)),
        grid=(nb, nq),
        in_specs=[tile(4 * HD), tile(LANE),
                  pl.BlockSpec((None, ncp, HD), lambda b, i: (b, 0, 0)),
                  pl.BlockSpec((None, ncp, HD), lambda b, i: (b, 0, 0)),
                  seq(2 * HD), seq(2 * HD), const(tb),
                  pl.BlockSpec((None, rows, ncp), lambda b, i: (i, 0, 0)),
                  const(cover), const(etab)],
        out_specs=pl.BlockSpec((QT, 4 * HD), lambda b, i: (b * nq + i, 0)),
        out_shape=jax.ShapeDtypeStruct((nb * s, 4 * HD), F32),
        compiler_params=_cparams(("parallel", "arbitrary")),
        name="nsa_prompt",
    )(qa, small, ck, cv, slc, win, tb, cb, cover, etab)


def _mla_prompt_kernel(q_ref, row_ref, inv_ref, wuv_ref, o_ref, m_s, l_s, acc_s):
    qi = pl.program_id(1)
    rows = MLA_HEADS * QT
    scale = MLA_QK ** -0.5
    q8 = jnp.concatenate([q_ref[:, ROWP * h:ROWP * (h + 1)] for h in range(MLA_HEADS)], axis=0)
    m_s[...] = jnp.full((rows, 1), NEG, F32)
    l_s[...] = jnp.zeros((rows, 1), F32)
    acc_s[...] = jnp.zeros((rows, KV_LORA), F32)

    def step(i, causal):
        kcs = [i * KC + u for u in range(KC)]
        tiles = [row_ref[pl.ds(pl.multiple_of(kc * QT, QT), QT), :].astype(BF16) for kc in kcs]
        inv = jnp.concatenate([inv_ref[kc] for kc in kcs], axis=1)
        mult = jnp.concatenate([jnp.broadcast_to(inv[h:h + 1], (QT, KC * QT)) for h in range(MLA_HEADS)], axis=0)
        sc = jnp.concatenate([_nt(q8, t) for t in tiles], axis=1) * (mult * scale)
        mask = jnp.concatenate([_tile_dist(qi, kc, rows) for kc in kcs], axis=1) >= 0 if causal else None
        m_, l_, a_ = _softmax_step(m_s[...], l_s[...], acc_s[...], sc, mask,
                                   _by_block([t[:, 0:KV_LORA] for t in tiles], _mm))
        m_s[...] = m_
        l_s[...] = l_
        acc_s[...] = a_

    def body(i, _):
        step(i, False)
        return 0

    lax.fori_loop(0, qi // KC, body, 0)
    step(qi // KC, True)
    o_lat = _finish(l_s[...], acc_s[...]).astype(BF16)
    for h in range(MLA_HEADS):
        o_ref[:, HD * h:HD * (h + 1)] = _mm(o_lat[QT * h:QT * (h + 1)], wuv_ref[h])


def _mla_prompt(qmla, rowp, invt, wuv, nb, s):
    nq = s // QT
    rows = MLA_HEADS * QT
    return pl.pallas_call(
        _mla_prompt_kernel,
        grid=(nb, nq),
        in_specs=[pl.BlockSpec((QT, MLA_HEADS * ROWP), lambda b, i: (b * nq + i, 0)),
                  pl.BlockSpec((s, ROWP), lambda b, i: (b, 0)),
                  pl.BlockSpec((None, nq, MLA_HEADS, QT), lambda b, i: (b, 0, 0, 0)),
                  pl.BlockSpec(wuv.shape, lambda b, i: (0, 0, 0))],
        out_specs=pl.BlockSpec((QT, MLA_HEADS * HD), lambda b, i: (b * nq + i, 0)),
        out_shape=jax.ShapeDtypeStruct((nb * s, MLA_HEADS * HD), F32),
        scratch_shapes=[pltpu.VMEM((rows, 1), F32), pltpu.VMEM((rows, 1), F32), pltpu.VMEM((rows, KV_LORA), F32)],
        compiler_params=_cparams(("parallel", "arbitrary")),
        name="mla_prompt",
    )(qmla, rowp, invt, wuv)


def _dsa_prompt_kernel(qc_ref, qidx_ref, small_ref, kidx_ref, kvc_ref, tb_ref, o_ref, key_s, *, ksel):
    qi = pl.program_id(1)
    rows = DSA_HEADS * QT
    scale = HD ** -0.5
    lane = lax.broadcasted_iota(jnp.int32, (QT, LANE), 1)
    w = small_ref[...]

    heads, wcols = [], []
    for j in range(IDX_HEADS // 2):
        x = qidx_ref[:, LANE * j:LANE * (j + 1)]
        heads += [jnp.where(lane < IDX_DIM, x, 0).astype(BF16), jnp.where(lane >= IDX_DIM, x, 0).astype(BF16)]
        wcols += [w[:, GATE_W + 2 * j:GATE_W + 2 * j + 1], w[:, GATE_W + 2 * j + 1:GATE_W + 2 * j + 2]]
    q16 = jnp.concatenate(heads, axis=0)
    wcol = jnp.concatenate(wcols, axis=0)
    n_groups = qi // KC + 1

    def idx_body(i, _):
        for u in range(KC):
            kc = i * KC + u
            kk = kidx_ref[pl.ds(pl.multiple_of(kc * QT, QT), QT), :]
            r = jnp.maximum(_nt(q16, kk), 0.0) * wcol
            acc = r[0:QT]
            for h in range(1, IDX_HEADS):
                acc = acc + r[QT * h:QT * (h + 1)]
            key_s[kc] = _order_key(jnp.where(_tile_dist(qi, kc, QT) >= 0, acc, -jnp.inf))
        return 0

    lax.fori_loop(0, n_groups, idx_body, 0)

    def count_ge(cand):
        def body(i, c):
            for u in range(KC):
                c = c + (key_s[i * KC + u] >= cand).astype(jnp.int32)
            return c
        c = lax.fori_loop(0, n_groups, body, jnp.zeros((QT, QT), jnp.int32))
        return jnp.sum(c, axis=-1, keepdims=True)

    thr = _kth_threshold(count_ge, QT, ksel)
    q4 = jnp.concatenate([qc_ref[:, HD * h:HD * (h + 1)] for h in range(DSA_HEADS)], axis=0)

    def att_body(i, carry):
        kcs = [i * KC + u for u in range(KC)]
        ks, vs = [], []
        for kc in kcs:
            kv = kvc_ref[pl.ds(pl.multiple_of(kc * QT, QT), QT), :]
            ks.append(kv[:, 0:HD].astype(BF16))
            vs.append(kv[:, HD:2 * HD].astype(BF16))
        sc = jnp.concatenate([_nt(q4, k) for k in ks], axis=1) * scale
        sc = sc + jnp.concatenate([tb_ref[jnp.clip(qi - kc, 0, 2)] for kc in kcs], axis=1)
        keep = jnp.concatenate([key_s[kc] >= thr for kc in kcs], axis=1)
        dist = jnp.concatenate([_tile_dist(qi, kc, rows) for kc in kcs], axis=1)
        keep = jnp.concatenate([keep] * DSA_HEADS, axis=0) & (dist >= 0)
        return _softmax_step(*carry, sc, keep, _by_block(vs, _mm))

    init = (jnp.full((rows, 1), NEG, F32), jnp.zeros((rows, 1), F32), jnp.zeros((rows, HD), F32))
    _, l, acc = lax.fori_loop(0, n_groups, att_body, init)
    o = _finish(l, acc)
    for h in range(DSA_HEADS):
        o_ref[:, HD * h:HD * (h + 1)] = o[QT * h:QT * (h + 1)]


def _dsa_prompt(qc, qidx, small, kidx2, kvc, tb, nb, s):
    nq = s // QT
    tile = lambda w: pl.BlockSpec((QT, w), lambda b, i: (b * nq + i, 0))
    return pl.pallas_call(
        functools.partial(_dsa_prompt_kernel, ksel=min(IDX_TOPK, s // 4)),
        grid=(nb, nq),
        in_specs=[tile(4 * HD), tile(IDX_HEADS * IDX_DIM), tile(LANE),
                  pl.BlockSpec((None, s, LANE), lambda b, i: (b, 0, 0)),
                  pl.BlockSpec((s, 2 * HD), lambda b, i: (b, 0)),
                  pl.BlockSpec(tb.shape, lambda b, i: (0, 0, 0))],
        out_specs=pl.BlockSpec((QT, 4 * HD), lambda b, i: (b * nq + i, 0)),
        out_shape=jax.ShapeDtypeStruct((nb * s, 4 * HD), F32),
        scratch_shapes=[pltpu.VMEM((nq, QT, QT), jnp.int32)],
        compiler_params=_cparams(("parallel", "arbitrary")),
        name="dsa_prompt",
    )(qc, qidx, small, kidx2, kvc, tb)


def _page_specs(layer, page_shape, n_groups, first_group=0):
    def index(b, g, pt, *, j):
        gg = jnp.clip(g - first_group, 0, n_groups - 1)
        return (layer, pt[b, gg * PG + j], 0, 0)
    return [pl.BlockSpec((None, None) + tuple(page_shape), functools.partial(index, j=j)) for j in range(PG)]


def _seq_spec(shape):
    return pl.BlockSpec((None,) + tuple(shape), lambda b, g, pt: (b,) + (0,) * len(shape))


def _const_spec(a):
    return pl.BlockSpec(a.shape, lambda b, g, pt: (0,) * a.ndim)


def _split_kv(ref):
    n = ref.shape[0] // 2
    return ref[pl.ds(0, n, stride=2), :].astype(BF16), ref[pl.ds(1, n, stride=2), :].astype(BF16)


def _split_kv_pages(pages):
    kvs = [_split_kv(p) for p in pages]
    return jnp.concatenate([k for k, _ in kvs], axis=0), jnp.concatenate([v for _, v in kvs], axis=0)


def _new_key_mask(rows, n_new):
    t = lax.broadcasted_iota(jnp.int32, (rows, PAGE), 0) % TP
    n = lax.broadcasted_iota(jnp.int32, (rows, PAGE), 1)
    return (n <= t) & (n < n_new)


def _s_cmp_kernel(pt_ref, q_ref, *refs, n_pages, n_new, n_pick):
    pages = refs[:PG]
    (new_ref, wh_ref, wt_ref, wn_ref, g_ref, cb_ref, cover_ref,
     ocmp_ref, sel_ref, ck_s, cv_s, carry_s) = refs[PG:]
    g = pl.program_id(1)
    rows = NSA_HEADS * TP
    nj = ck_s.shape[0]
    past = n_pages * PAGE
    row8 = lax.broadcasted_iota(jnp.int32, (8, HD), 0)

    @pl.when(g == 0)
    def _():
        ck_s[...] = jnp.zeros(ck_s.shape, F32)
        cv_s[...] = jnp.zeros(cv_s.shape, F32)
        carry_s[...] = jnp.zeros(carry_s.shape, F32)

    def pair_sums(x):
        y = x[0:8] + x[8:16] + x[16:24] + x[24:32]
        y = y + pltpu.roll(y, 4, 0)
        return y + pltpu.roll(y, 2, 0)

    rows_per_chunk = 2 * CMP_STRIDE
    head = carry_s[...]
    for j in range(PG):
        ck8 = jnp.zeros((8, HD), F32)
        cv8 = jnp.zeros((8, HD), F32)
        for c in range(PAGE // CMP_STRIDE):
            x = pages[j][rows_per_chunk * c:rows_per_chunk * (c + 1), :]
            comp = pair_sums(x * wt_ref[...]) + head
            head = pair_sums(x * wh_ref[...])
            ck8 = jnp.where(row8 == c, comp[0:1], ck8)
            cv8 = jnp.where(row8 == c, comp[1:2], cv8)
        off = pl.multiple_of(g * (PG * 8) + 8 * j, 8)
        ck_s[pl.ds(off, 8), :] = ck8
        cv_s[pl.ds(off, 8), :] = cv8
    carry_s[...] = head
    carry_k = head[0:1]
    carry_v = head[1:2]

    @pl.when(g == pl.num_programs(1) - 1)
    def _():
        new = new_ref[...]
        tail_k = jnp.sum(new[:, 0:HD] * wn_ref[0], axis=0, keepdims=True) + carry_k
        tail_v = jnp.sum(new[:, HD:2 * HD] * wn_ref[1], axis=0, keepdims=True) + carry_v
        last = 8 * n_pages
        ck_s[last:last + 8, :] = jnp.where(row8 == 0, tail_k, 0.0)
        cv_s[last:last + 8, :] = jnp.where(row8 == 0, tail_v, 0.0)

        ck = ck_s[...]
        ms = jnp.mean(ck * ck, axis=-1, keepdims=True)
        ckn = (ck * lax.rsqrt(ms + EPS) * g_ref[...]).astype(BF16)
        s = _nt(q_ref[...], ckn) * HD ** -0.5 + cb_ref[...]
        t = lax.broadcasted_iota(jnp.int32, (rows, nj), 0) % TP
        jj = lax.broadcasted_iota(jnp.int32, (rows, nj), 1)
        mask = (jj >= 1) & (jj <= last) & (past + t - CMP_STRIDE * (jj - 1) - (CMP_LEN - 1) >= 0)
        s = jnp.where(mask, s, NEG)
        e = jnp.where(mask, jnp.exp(s - jnp.max(s, axis=-1, keepdims=True)), 0.0)
        p = e / jnp.maximum(jnp.sum(e, axis=-1, keepdims=True), 1e-30)
        ocmp_ref[...] = _mm(p.astype(BF16), cv_s[...].astype(BF16))
        psum = p[0:TP] + p[TP:2 * TP] + p[2 * TP:3 * TP] + p[3 * TP:4 * TP]
        imp = jnp.dot(psum, cover_ref[...], preferred_element_type=F32, precision=lax.Precision.HIGHEST)
        nsp = imp.shape[1]
        blk = lax.broadcasted_iota(jnp.int32, (TP, nsp), 1)
        qb = (past + lax.broadcasted_iota(jnp.int32, (TP, nsp), 0)) // SEL_BLOCK
        forced = (blk == 0) | (blk == qb) | (blk == qb - 1)
        score = jnp.where(blk <= qb, imp + jnp.where(forced, FORCE_BONUS, 0.0), -jnp.inf)
        sel_ref[...] = _top_blocks(score, n_pick)


def _s_slc_kernel(pt_ref, q_ref, sel_ref, ocmp_ref, small_ref, *refs, n_pages, n_new):
    pages = refs[:PG]
    (nslc_ref, win_ref, nwin_ref, tb_ref, wb_ref, o_ref, m_s, l_s, acc_s) = refs[PG:]
    g = pl.program_id(1)
    rows = NSA_HEADS * TP
    scale = HD ** -0.5
    q = q_ref[...]
    sel = sel_ref[...]
    nsp = sel.shape[1]
    lane_s = lax.broadcasted_iota(jnp.int32, (TP, nsp), 1)
    lane_k = lax.broadcasted_iota(jnp.int32, (TP, PAGE), 1)

    @pl.when(g == 0)
    def _():
        m_s[...] = jnp.full(m_s.shape, NEG, F32)
        l_s[...] = jnp.zeros(l_s.shape, F32)
        acc_s[...] = jnp.zeros(acc_s.shape, F32)

    def block_col(b):
        return jnp.sum(jnp.where(lane_s == b, sel, 0.0), axis=-1, keepdims=True)

    def picked(first_block):
        half = jnp.where(lane_k < SEL_BLOCK, block_col(first_block), block_col(first_block + 1))
        return jnp.concatenate([half] * NSA_HEADS, axis=0) > 0.5

    k_all, v_all = _split_kv_pages(pages)
    last = g == pl.num_programs(1) - 1
    bias = [tb_ref[2]] * (PG - 1) + [tb_ref[jnp.where(last, 0, 2)]]
    sc = _nt(q, k_all) * scale + jnp.concatenate(bias, axis=1)
    keep = jnp.concatenate([picked(2 * (g * PG + j)) for j in range(PG)], axis=1)
    state = _softmax_step(m_s[...], l_s[...], acc_s[...], sc, keep, v_all)
    m_s[...], l_s[...], acc_s[...] = state

    @pl.when(g == pl.num_programs(1) - 1)
    def _():
        new_ok = _new_key_mask(rows, n_new)
        kv = nslc_ref[...]
        sc = _nt(q, kv[:, 0:HD].astype(BF16)) * scale + tb_ref[1]
        _, l, acc = _softmax_step(*state, sc, picked(2 * n_pages) & new_ok, kv[:, HD:2 * HD].astype(BF16))
        o_slc = _finish(l, acc)

        wbuf = win_ref.shape[0] // 2
        k, v = _split_kv(win_ref)
        sc = _nt(q, k) * scale + wb_ref[...]
        t = lax.broadcasted_iota(jnp.int32, (rows, wbuf), 0) % TP
        n = lax.broadcasted_iota(jnp.int32, (rows, wbuf), 1)
        init = (jnp.full((rows, 1), NEG, F32), jnp.zeros((rows, 1), F32), jnp.zeros((rows, HD), F32))
        st = _softmax_step(*init, sc, wbuf + t - n <= WINDOW, v)
        kv = nwin_ref[...]
        sc = _nt(q, kv[:, 0:HD].astype(BF16)) * scale + tb_ref[1]
        _, l, acc = _softmax_step(*st, sc, new_ok, kv[:, HD:2 * HD].astype(BF16))
        o_win = _finish(l, acc)

        gate = small_ref[...]
        o_cmp = ocmp_ref[...]
        for h in range(NSA_HEADS):
            r = slice(TP * h, TP * (h + 1))
            o_ref[:, HD * h:HD * (h + 1)] = (gate[:, 3 * h:3 * h + 1] * o_cmp[r] + gate[:, 3 * h + 1:3 * h + 2] * o_slc[r]
                                             + gate[:, 3 * h + 2:3 * h + 3] * o_win[r])


def _s_mla_kernel(pt_ref, q_ref, *refs, n_new):
    pages = refs[:PG]
    (new_ref, wuv_ref, o_ref, m_s, l_s, acc_s) = refs[PG:]
    g = pl.program_id(1)
    rows = MLA_HEADS * TP
    scale = MLA_QK ** -0.5
    n_qk = KV_LORA + MLA_ROPE
    q = q_ref[:, 0:n_qk]

    @pl.when(g == 0)
    def _():
        m_s[...] = jnp.full(m_s.shape, NEG, F32)
        l_s[...] = jnp.zeros(l_s.shape, F32)
        acc_s[...] = jnp.zeros(acc_s.shape, F32)

    def update(state, page_refs, mask):
        kt = jnp.concatenate([r[0:n_qk, :] for r in page_refs], axis=1).astype(BF16)
        inv = jnp.concatenate([r[n_qk:MLA_ROW, :] for r in page_refs], axis=1)
        mult = jnp.concatenate([jnp.broadcast_to(inv[h:h + 1], (TP, inv.shape[1])) for h in range(MLA_HEADS)], axis=0)
        sc = _mm(q, kt) * (mult * scale)
        return _softmax_step(*state, sc, mask, lambda e: _nt(e, kt[0:KV_LORA]))

    state = update((m_s[...], l_s[...], acc_s[...]), pages, None)
    m_s[...], l_s[...], acc_s[...] = state

    @pl.when(g == pl.num_programs(1) - 1)
    def _():
        _, l, acc = update(state, [new_ref], _new_key_mask(rows, n_new))
        o_lat = _finish(l, acc).astype(BF16)
        for h in range(MLA_HEADS):
            o_ref[:, HD * h:HD * (h + 1)] = _mm(o_lat[TP * h:TP * (h + 1)], wuv_ref[h])


def _s_dsa_kernel(pt_ref, qidx_ref, wb_ref, qc_ref, *refs, n_pages, n_new, ksel):
    kpages = refs[:PG]
    vpages = refs[PG:2 * PG]
    (nkidx_ref, nkv_ref, tb_ref, o_ref, key_s, thr_s, m_s, l_s, acc_s) = refs[2 * PG:]
    g = pl.program_id(1)
    ng = n_pages // PG
    rows = DSA_HEADS * TP
    scale = HD ** -0.5

    def idx_scores(kt):
        r = jnp.maximum(_mm(qidx_ref[...], kt.astype(BF16)), 0.0) * wb_ref[...]
        sc = r[0:TP]
        for h in range(1, IDX_HEADS):
            sc = sc + r[TP * h:TP * (h + 1)]
        return sc

    @pl.when(g == 0)
    def _():
        m_s[...] = jnp.full(m_s.shape, NEG, F32)
        l_s[...] = jnp.zeros(l_s.shape, F32)
        acc_s[...] = jnp.zeros(acc_s.shape, F32)

    @pl.when(g < ng)
    def _():
        keys = _order_key(idx_scores(jnp.concatenate([p[...] for p in kpages], axis=1)))
        for j in range(PG):
            key_s[g * PG + j] = keys[:, PAGE * j:PAGE * (j + 1)]

    @pl.when(g == ng - 1)
    def _():
        sc = jnp.where(_new_key_mask(TP, n_new), idx_scores(nkidx_ref[...]), -jnp.inf)
        key_s[n_pages] = _order_key(sc)

        def count_ge(cand):
            c = jnp.sum((key_s[...] >= cand[None]).astype(jnp.int32), axis=0)
            return jnp.sum(c, axis=-1, keepdims=True)

        thr_s[...] = jnp.broadcast_to(_kth_threshold(count_ge, TP, ksel), thr_s.shape)

    def kept(pidx):
        return jnp.concatenate([key_s[pidx] >= thr_s[...]] * DSA_HEADS, axis=0)

    @pl.when(g >= ng)
    def _():
        k_all, v_all = _split_kv_pages(vpages)
        bias = [tb_ref[2]] * (PG - 1) + [tb_ref[jnp.where(g == 2 * ng - 1, 0, 2)]]
        sc = _nt(qc_ref[...], k_all) * scale + jnp.concatenate(bias, axis=1)
        keep = jnp.concatenate([kept((g - ng) * PG + j) for j in range(PG)], axis=1)
        state = _softmax_step(m_s[...], l_s[...], acc_s[...], sc, keep, v_all)
        m_s[...], l_s[...], acc_s[...] = state

    @pl.when(g == 2 * ng - 1)
    def _():
        kv = nkv_ref[...]
        sc = _nt(qc_ref[...], kv[:, 0:HD].astype(BF16)) * scale + tb_ref[1]
        _, l, acc = _softmax_step(m_s[...], l_s[...], acc_s[...], sc, kept(n_pages) & _new_key_mask(rows, n_new),
                                  kv[:, HD:2 * HD].astype(BF16))
        o = _finish(l, acc)
        for h in range(DSA_HEADS):
            o_ref[:, HD * h:HD * (h + 1)] = o[TP * h:TP * (h + 1)]


def _paged_call(body, page_table, ins, in_specs, out_shapes, out_specs, scratch, n_steps):
    nb = page_table.shape[0]
    return pl.pallas_call(
        body,
        grid_spec=pltpu.PrefetchScalarGridSpec(
            num_scalar_prefetch=1, grid=(nb, n_steps), in_specs=in_specs, out_specs=out_specs,
            scratch_shapes=scratch),
        out_shape=out_shapes,
        compiler_params=_cparams(("parallel", "arbitrary")),
        name=body.func.__name__.strip("_"),
    )(page_table, *ins)


def _softmax_scratch(rows, width):
    return [pltpu.VMEM((rows, 1), F32), pltpu.VMEM((rows, 1), F32), pltpu.VMEM((rows, width), F32)]


def _sample_attention(layer, page_table, n_new, q_a, q_c, q_idx, q_mla, small, w_idx, new, pools, state_win, tabs, wts):
    nb, n_pages = page_table.shape
    ng = n_pages // PG
    past = n_pages * PAGE
    rows4 = NSA_HEADS * TP
    nj = tabs["cb_s"].shape[1]
    nsp = tabs["cover_s"].shape[1]
    kv_page = (2 * PAGE, HD)
    kv_new = (PAGE, 2 * HD)

    ins = ([q_a] + [pools["cmp"]] * PG
           + [new["cmp"], wts["wh"], wts["wt"], wts["wn"], wts["g_cmp"], tabs["cb_s"], tabs["cover_s"]])
    specs = ([_seq_spec((rows4, HD))] + _page_specs(layer, kv_page, ng)
             + [_seq_spec((TP, 2 * HD))] + [_const_spec(a) for a in ins[PG + 2:]])
    o_cmp, sel = _paged_call(
        functools.partial(_s_cmp_kernel, n_pages=n_pages, n_new=n_new, n_pick=min(N_SEL, 2 * n_pages + 1)),
        page_table, ins, specs,
        [jax.ShapeDtypeStruct((nb, rows4, HD), F32), jax.ShapeDtypeStruct((nb, TP, nsp), F32)],
        [_seq_spec((rows4, HD)), _seq_spec((TP, nsp))],
        [pltpu.VMEM((nj, HD), F32), pltpu.VMEM((nj, HD), F32), pltpu.VMEM((8, HD), F32)], ng)

    ins = ([q_a, sel, o_cmp, small] + [pools["slc"]] * PG
           + [new["slc"], state_win, new["win"], tabs["tbs_a"], tabs["wb"]])
    specs = ([_seq_spec((rows4, HD)), _seq_spec((TP, nsp)), _seq_spec((rows4, HD)), _seq_spec((TP, LANE))]
             + _page_specs(layer, kv_page, ng)
             + [_seq_spec(kv_new),
                pl.BlockSpec((None, None) + state_win.shape[2:], lambda b, g, pt: (layer, b, 0, 0)),
                _seq_spec(kv_new), _const_spec(tabs["tbs_a"]), _const_spec(tabs["wb"])])
    o_a = _paged_call(
        functools.partial(_s_slc_kernel, n_pages=n_pages, n_new=n_new),
        page_table, ins, specs, jax.ShapeDtypeStruct((nb, TP, NSA_HEADS * HD), F32),
        _seq_spec((TP, NSA_HEADS * HD)), _softmax_scratch(rows4, HD), ng)

    rows8 = MLA_HEADS * TP
    ins = [q_mla] + [pools["mla"]] * PG + [new["mla"], wts["wuv"]]
    specs = ([_seq_spec((rows8, ROWP))] + _page_specs(layer, (MLA_ROW, PAGE), ng)
             + [_seq_spec((MLA_ROW, PAGE)), _const_spec(wts["wuv"])])
    o_b = _paged_call(
        functools.partial(_s_mla_kernel, n_new=n_new),
        page_table, ins, specs, jax.ShapeDtypeStruct((nb, TP, MLA_HEADS * HD), F32),
        _seq_spec((TP, MLA_HEADS * HD)), _softmax_scratch(rows8, KV_LORA), ng)

    rows16 = IDX_HEADS * TP
    ins = ([q_idx, w_idx, q_c] + [pools["kidx"]] * PG + [pools["dkv"]] * PG
           + [new["kidx"], new["dkv"], tabs["tbs_c"]])
    specs = ([_seq_spec((rows16, IDX_DIM)), _seq_spec((rows16, 1)), _seq_spec((rows4, HD))]
             + _page_specs(layer, (IDX_DIM, PAGE), ng) + _page_specs(layer, kv_page, ng, first_group=ng)
             + [_seq_spec((IDX_DIM, PAGE)), _seq_spec(kv_new), _const_spec(tabs["tbs_c"])])
    o_c = _paged_call(
        functools.partial(_s_dsa_kernel, n_pages=n_pages, n_new=n_new, ksel=min(IDX_TOPK, (past + n_new) // 4)),
        page_table, ins, specs, jax.ShapeDtypeStruct((nb, TP, DSA_HEADS * HD), F32),
        _seq_spec((TP, DSA_HEADS * HD)),
        [pltpu.VMEM((n_pages + 1, TP, PAGE), jnp.int32), pltpu.VMEM((TP, PAGE), jnp.int32)]
        + _softmax_scratch(rows4, HD), 2 * ng)
    return o_a, o_b, o_c


def _t5_bucket(n):
    max_exact = NUM_BUCKETS // 2
    nf = jnp.maximum(n, 1).astype(F32)
    large = max_exact + (jnp.log(nf / max_exact) / math.log(MAX_DISTANCE / max_exact)
                         * (NUM_BUCKETS - max_exact)).astype(jnp.int32)
    return jnp.where(n < max_exact, n, jnp.minimum(large, NUM_BUCKETS - 1))


def _round_up(x, m):
    return -(-x // m) * m


def _cover(n_rows, n_cols, first_row, n_valid, n_blocks):
    n = np.arange(n_rows)[:, None] - first_row
    s = np.arange(n_cols)[None]
    shared = np.maximum(np.minimum(CMP_STRIDE * n + CMP_LEN, SEL_BLOCK * s + SEL_BLOCK)
                        - np.maximum(CMP_STRIDE * n, SEL_BLOCK * s), 0)
    ok = (n >= 0) & (n < n_valid) & (s < n_blocks)
    return jnp.asarray(np.where(ok, shared / CMP_LEN, 0.0), F32)


def _tables(rel_bias, s, n_pages, n_new, wbuf):
    nq = s // QT
    past = n_pages * PAGE
    def look(heads, dist):
        hit = _t5_bucket(dist)[..., None] == jnp.arange(NUM_BUCKETS)
        rb = rel_bias[:, heads].astype(F32).T
        rb = rb.reshape((rb.shape[0],) + (1,) * dist.ndim + (NUM_BUCKETS,))
        return jnp.sum(jnp.where(hit[None], rb, 0.0), axis=-1)

    nsa, dsa = slice(0, NSA_HEADS), slice(NSA_HEADS, NSA_HEADS + DSA_HEADS)
    t = jnp.arange(QT)[:, None]
    c = jnp.arange(QT)[None]
    tb = lambda heads: jnp.stack([look(heads, QT * d + t - c) for d in range(3)]).reshape(3, -1, QT)
    ncp = s // CMP_STRIDE
    cb_p = look(nsa, jnp.arange(s)[:, None] - CMP_STRIDE * jnp.arange(ncp)[None] - (CMP_LEN - 1))
    cb_p = cb_p.reshape(NSA_HEADS, nq, QT, ncp).transpose(1, 0, 2, 3).reshape(nq, NSA_HEADS * QT, ncp)

    t8 = jnp.arange(TP)[:, None]
    n = jnp.arange(PAGE)[None]
    tbs = lambda heads: jnp.stack([look(heads, PAGE + t8 - n), look(heads, t8 - n),
                                   look(heads, jnp.full((TP, PAGE), FAR))]).reshape(3, -1, PAGE)
    nj = _round_up(8 * n_pages + 8, LANE)
    n_sel_blocks = -(-(past + n_new) // SEL_BLOCK)
    etab = (np.arange(s // SEL_BLOCK)[None, :, None]
            == (np.arange(nq)[:, None, None] * QT + np.arange(QT)[None, None, :]) // SEL_BLOCK)
    return dict(
        tb_a=tb(nsa), tb_c=tb(dsa), cb_p=cb_p,
        cover_p=_cover(ncp, s // SEL_BLOCK, 0, ncp - 1, s // SEL_BLOCK),
        etab=jnp.asarray(etab, BF16),
        tbs_a=tbs(nsa), tbs_c=tbs(dsa),
        wb=look(nsa, wbuf + t8 - jnp.arange(wbuf)[None]).reshape(-1, wbuf),
        cb_s=look(nsa, past + t8 - CMP_STRIDE * (jnp.arange(nj)[None] - 1) - (CMP_LEN - 1)).reshape(-1, nj),
        cover_s=_cover(nj, _round_up(n_sel_blocks, LANE), 1, 8 * n_pages, n_sel_blocks),
    )


_PROJ_SRC = np.concatenate([
    np.arange(0, 512), np.arange(512, 1280),
    np.arange(2124, 2636), np.arange(2636, 2892),
    np.arange(2892, 3916),
    np.arange(1292, 1804), np.arange(1804, 2060),
    np.arange(2060, 2124), np.arange(3916, 3980),
    np.arange(1280, 1292), np.arange(3980, 3996), np.full(LANE - GATE_W - IDX_HEADS, -1),
])


def _layer_weights(l, w_in, w_nsa_cmp, g_nsa_qk, g_mla_qlat, g_mla_kvlat, w_mla_uq, w_mla_uk, w_mla_uv,
                   g_mla_q, g_mla_k, g_dsa_qk, s):
    src = jnp.asarray(np.maximum(_PROJ_SRC, 0))
    w_in_p = jnp.where(jnp.asarray(_PROJ_SRC >= 0)[None], jnp.take(w_in[l], src, axis=1), 0.0).astype(BF16)

    def row(v):
        return jnp.pad(v.astype(F32), (0, Q_LORA - v.shape[0]))
    zeros = jnp.zeros((LANE - MLA_ROPE,), F32)
    gpack = jnp.stack([
        row(g_nsa_qk[l, 0]), row(g_nsa_qk[l, 2]), row(g_nsa_qk[l, 3]), row(g_dsa_qk[l, 0]), row(g_dsa_qk[l, 1]),
        row(g_mla_qlat[l]), row(g_mla_kvlat[l]), row(jnp.concatenate([g_mla_q[l], zeros])),
        row(g_mla_k[l, :MLA_NOPE]), row(jnp.concatenate([g_mla_k[l, MLA_NOPE:], zeros]))]
        + [jnp.zeros((Q_LORA,), F32)] * 6)

    wc = w_nsa_cmp[l].astype(F32)
    ncp = s // CMP_STRIDE
    off = np.arange(s)[None] - CMP_STRIDE * np.arange(ncp)[:, None]
    ok = (off >= 0) & (off < CMP_LEN) & (np.arange(ncp)[:, None] < ncp - 1)
    band = lambda k: jnp.where(jnp.asarray(ok), wc[k][jnp.asarray(np.clip(off, 0, CMP_LEN - 1))], 0.0).astype(BF16)
    rows_w = lambda w: jnp.broadcast_to(w.T.reshape(2 * CMP_STRIDE, 1), (2 * CMP_STRIDE, HD))
    wh, wt = rows_w(wc[:, :CMP_STRIDE]), rows_w(wc[:, CMP_STRIDE:])
    wn = jnp.broadcast_to(wc[:, CMP_STRIDE:CMP_STRIDE + TP, None], (2, TP, HD))

    return dict(
        w_in=w_in_p, gpack=gpack,
        wuq2=jnp.pad(w_mla_uq[l], ((0, 0), (0, 0), (0, LANE - MLA_ROPE))).reshape(Q_LORA, -1).astype(BF16),
        wukt=w_mla_uk[l].transpose(1, 2, 0).astype(BF16),
        wukf=w_mla_uk[l].reshape(KV_LORA, -1).astype(BF16),
        wuv=w_mla_uv[l].transpose(1, 0, 2).astype(BF16),
        band_k=band(0), band_v=band(1), wh=wh, wt=wt, wn=wn,
        g_cmp=g_nsa_qk[l, 1].reshape(1, HD).astype(F32),
    )


def kernel(x_prompt, x_sample, cache_nsa_cmp_kv, cache_nsa_slc_kv, state_nsa_win_kv, cache_mla, cache_dsa_kv, cache_dsa_kidx, page_table, rel_bias, g_mix, w_in, g_nsa_qk, w_nsa_cmp, g_mla_qlat, g_mla_kvlat, w_mla_uq, w_mla_uk, w_mla_uv, g_mla_q, g_mla_k, g_dsa_qk, w_out, g_ffn, w_up, w_down):
    nb, s, _ = x_prompt.shape
    nd, n_new, _ = x_sample.shape
    depth, n_pool = cache_mla.shape[:2]
    n_pages = page_table.shape[1]
    past = n_pages * PAGE
    wbuf = state_nsa_win_kv.shape[2]
    assert s % (KC * QT) == 0 and n_pages % PG == 0 and n_new <= TP and wbuf >= WINDOW
    mp, ms = nb * s, nd * n_new
    m_pad = _round_up(mp + ms, MM_T)
    pad_rows = lambda a: jnp.pad(a, ((0, m_pad - a.shape[0]), (0, 0)))
    smp = slice(mp, mp + ms)

    x = pad_rows(jnp.concatenate([x_prompt.reshape(mp, D_MODEL), x_sample.reshape(ms, D_MODEL)]))
    pos = jnp.concatenate([jnp.tile(jnp.arange(s), nb), jnp.tile(past + jnp.arange(n_new), nd)]).astype(F32)
    half = MLA_ROPE // 2
    ang = pos[:, None] * (ROPE_THETA ** (-jnp.arange(half, dtype=F32) / half))[None]
    cos, sin, z = jnp.cos(ang), jnp.sin(ang), jnp.zeros((mp + ms, LANE - MLA_ROPE), F32)
    cs = pad_rows(jnp.concatenate([cos, cos, z, -sin, sin, z], axis=1))

    tabs = _tables(rel_bias, s, n_pages, n_new, wbuf)
    pools = dict(
        cmp=cache_nsa_cmp_kv.reshape(depth, n_pool, 2 * PAGE, HD),
        slc=cache_nsa_slc_kv.reshape(depth, n_pool, 2 * PAGE, HD),
        dkv=cache_dsa_kv.reshape(depth, n_pool, 2 * PAGE, HD),
        kidx=jnp.swapaxes(cache_dsa_kidx, -1, -2), mla=jnp.swapaxes(cache_mla, -1, -2))
    state_win = state_nsa_win_kv.reshape(depth, nd, 2 * wbuf, HD)

    def per_seq(a, rows):
        a = a.reshape(nd, n_new, a.shape[-1])
        return jnp.pad(a, ((0, 0), (0, rows - n_new), (0, 0)))

    def heads_first(a, n_heads):
        a = per_seq(a, TP).reshape(nd, TP, n_heads, -1)
        return a.transpose(0, 2, 1, 3).reshape(nd, n_heads * TP, -1)

    prompt_new, sample_new = [], []
    for l in range(depth):
        wts = _layer_weights(l, w_in, w_nsa_cmp, g_nsa_qk, g_mla_qlat, g_mla_kvlat, w_mla_uq, w_mla_uk,
                             w_mla_uv, g_mla_q, g_mla_k, g_dsa_qk, s)
        proj = _mm_full(x, wts["w_in"], gain=g_mix[l])
        qa, slc, win, kvc, qc, qidx, small, qmla, rowp = _post(
            proj, cs, wts["gpack"], wts["wuq2"], wts["wukt"], wts["wukf"])
        kvcmp = proj[:, C_CMP:C_CMP + 2 * HD]
        kidx = proj[:, C_A + MLA_ROPE:C_A + LANE]
        inv_r = rowp[:, KV_LORA + MLA_ROPE:MLA_ROW]

        ck, cv = _compress_prompt(kvcmp, wts["band_k"], wts["band_v"], wts["g_cmp"], nb, s)
        o_a = _nsa_prompt(qa, small, ck, cv, slc, win, tabs["tb_a"], tabs["cb_p"], tabs["cover_p"],
                          tabs["etab"], nb, s)
        inv_p = jnp.swapaxes(inv_r[:mp].reshape(nb, s // QT, QT, MLA_HEADS), -1, -2)
        o_b = _mla_prompt(qmla, rowp, inv_p, wts["wuv"], nb, s)
        kidx2 = jnp.concatenate([kidx[:mp], kidx[:mp]], axis=1).reshape(nb, s, LANE).astype(BF16)
        o_c = _dsa_prompt(qc, qidx, small, kidx2, kvc, tabs["tb_c"], nb, s)

        small_s = per_seq(small[smp], TP)
        w_idx = small_s[:, :, GATE_W:GATE_W + IDX_HEADS].transpose(0, 2, 1).reshape(nd, IDX_HEADS * TP, 1)
        new = dict(cmp=per_seq(kvcmp[smp], TP), slc=per_seq(slc[smp], PAGE), win=per_seq(win[smp], PAGE),
                   mla=jnp.swapaxes(per_seq(rowp[smp, :MLA_ROW], PAGE), -1, -2),
                   kidx=jnp.swapaxes(per_seq(kidx[smp], PAGE), -1, -2), dkv=per_seq(kvc[smp], PAGE))
        so_a, so_b, so_c = _sample_attention(
            l, page_table, n_new, heads_first(qa[smp], NSA_HEADS), heads_first(qc[smp], DSA_HEADS),
            heads_first(qidx[smp], IDX_HEADS), heads_first(qmla[smp], MLA_HEADS), small_s,
            w_idx, new, pools, state_win, tabs, wts)

        mixed = pad_rows(jnp.concatenate([
            jnp.concatenate([o_a, o_b, o_c], axis=1),
            jnp.concatenate([so_a[:, :n_new], so_b[:, :n_new], so_c[:, :n_new]], axis=2).reshape(ms, -1)]))
        x1 = _mm_full(mixed, w_out[l].astype(BF16), res=x)
        hidden = _mm_full(x1, w_up[l].astype(BF16), gain=g_ffn[l], act="relu2", out_dtype=BF16)
        x = _mm_ktiled(hidden, w_down[l].astype(BF16), x1)

        kv4 = lambda a, r, n: a[r].reshape(n, -1, 2, HD)
        pr = slice(0, mp)
        prompt_new.append((kv4(kvcmp, pr, nb), kv4(slc, pr, nb), kv4(win, pr, nb)[:, -min(WINDOW, s):],
                           rowp[pr, :MLA_ROW].reshape(nb, s, MLA_ROW), kv4(kvc, pr, nb),
                           kidx[pr].reshape(nb, s, IDX_DIM)))
        win_all = jnp.concatenate([state_nsa_win_kv[l], kv4(win, smp, nd)], axis=1)
        sample_new.append((kv4(kvcmp, smp, nd), kv4(slc, smp, nd), win_all[:, -min(WINDOW, past + n_new):],
                           rowp[smp, :MLA_ROW].reshape(nd, n_new, MLA_ROW), kv4(kvc, smp, nd),
                           kidx[smp].reshape(nd, n_new, IDX_DIM)))

    outs_p = [jnp.stack(a) for a in zip(*prompt_new)]
    outs_s = [jnp.stack(a) for a in zip(*sample_new)]
    return (x[:mp].reshape(nb, s, D_MODEL), x[smp].reshape(nd, n_new, D_MODEL), *outs_p, *outs_s)
```
